```python
import jax, jax.numpy as jnp
from jax import lax
import numpy as np

D_MODEL = 2048
BATCH = 2
SEQ = 4096
DEPTH = 4
DEC_BATCH = 128
DEC_SEQ = 4
PAST_LEN = 8192
PAGE_SIZE = 128

N_MIXERS = 4
HEAD_DIM = 128
Q_BLOCK = 128
DEEPNORM_ALPHA = (2 * DEPTH) ** 0.25
DEEPNORM_OUT_SCALE = (8 * DEPTH) ** -0.25
LN_EPS = 1e-5
RMS_EPS = 1e-6

MLA_HEADS = D_MODEL // HEAD_DIM
MLA_NOPE = HEAD_DIM
MLA_ROPE = HEAD_DIM // 2
MLA_V = HEAD_DIM
MLA_Q_LORA = D_MODEL // 4
MLA_KV_LORA = D_MODEL // 4
ROPE_THETA = 10000.0
SB_HEADS = D_MODEL // HEAD_DIM
SB_KV_HEADS = SB_HEADS // 4
FOX_HEADS = D_MODEL // HEAD_DIM
FOX_KV_HEADS = FOX_HEADS // 4
ML_HEADS = 4
ML_V = D_MODEL // ML_HEADS
ML_QK = ML_V // 2
ML_CHUNK = 64

kernel_name = 'hybrid_mla_stickbreak_fox_mlstm_step'


def _layernorm(x, g, b):
    xf = x.astype(jnp.float32)
    mu = jnp.mean(xf, -1, keepdims=True)
    var = jnp.mean(jnp.square(xf - mu), -1, keepdims=True)
    return ((xf - mu) * lax.rsqrt(var + LN_EPS) * g.astype(jnp.float32) + b.astype(jnp.float32)).astype(x.dtype)


def _rmsnorm(x, g):
    xf = x.astype(jnp.float32)
    return (xf * lax.rsqrt(jnp.mean(jnp.square(xf), -1, keepdims=True) + RMS_EPS) * g.astype(jnp.float32)).astype(x.dtype)


def _rope(x, pos):
    half = x.shape[-1] // 2
    inv = ROPE_THETA ** (-jnp.arange(half, dtype=jnp.float32) / half)
    ang = pos.astype(jnp.float32)[:, None] * inv[None, :]
    cos = jnp.cos(ang)[None, :, None, :]
    sin = jnp.sin(ang)[None, :, None, :]
    xf = x.astype(jnp.float32)
    x1, x2 = xf[..., :half], xf[..., half:]
    return jnp.concatenate([x1 * cos - x2 * sin, x1 * sin + x2 * cos], axis=-1).astype(x.dtype)


def _gather_pages(cache, page_table):
    g = cache[page_table]
    return g.reshape((g.shape[0], g.shape[1] * g.shape[2]) + g.shape[3:])


def _sweep_query_blocks(fn, qs, q_pos):
    b, t = qs[0].shape[:2]
    nb = t // Q_BLOCK
    blocks = tuple(jnp.moveaxis(q.reshape((b, nb, Q_BLOCK) + q.shape[2:]), 1, 0) for q in qs)
    out = lax.map(lambda a: fn(*a), blocks + (q_pos.reshape(nb, Q_BLOCK),))
    return jnp.moveaxis(out, 0, 1).reshape((b, t) + out.shape[3:])


def _split_by_segments(w, segs):
    parts, off = [], 0
    for seg in segs:
        n = seg[0].shape[1]
        parts.append(w[..., off:off + n])
        off += n
    return parts


def _gqa_scores(q, k):
    return jnp.einsum('btkgd,bskd->bkgts', q, k)


def _gqa_mix(w, v):
    return jnp.einsum('bkgts,bskd->btkgd', w, v)


def _mla_attend(q_lat, q_rope, q_pos, segs):
    scale = (MLA_NOPE + MLA_ROPE) ** -0.5
    s = jnp.concatenate([jnp.einsum('bthc,bsc->bhts', q_lat, c) + jnp.einsum('bthr,bsr->bhts', q_rope, r)
                         for c, r, _ in segs], axis=-1).astype(jnp.float32) * scale
    k_pos = jnp.concatenate([p for _, _, p in segs])
    causal = k_pos[None, :] <= q_pos[:, None]
    p = jax.nn.softmax(jnp.where(causal, s, -jnp.inf), axis=-1).astype(q_lat.dtype)
    return sum(jnp.einsum('bhts,bsc->bthc', pw, seg[0]) for pw, seg in zip(_split_by_segments(p, segs), segs))


def _mla_group(x, pos, past, w_in, g_cq, g_ckv, w_uq, w_uk, w_uv, w_out):
    b, t, _ = x.shape
    cq, ckv, kr, gate = jnp.split(x @ w_in, [MLA_Q_LORA, MLA_Q_LORA + MLA_KV_LORA,
                                             MLA_Q_LORA + MLA_KV_LORA + MLA_ROPE], axis=-1)
    cq = _rmsnorm(cq, g_cq)
    ckv = _rmsnorm(ckv, g_ckv)
    q = jnp.einsum('btc,chd->bthd', cq, w_uq)
    q_rope = _rope(q[..., MLA_NOPE:], pos)
    kr = _rope(kr[:, :, None, :], pos)[:, :, 0, :]
    q_lat = jnp.einsum('bthd,chd->bthc', q[..., :MLA_NOPE], w_uk)
    if past is None:
        segs = ((ckv, kr, pos),)
        o_lat = _sweep_query_blocks(lambda ql, qr, pb: _mla_attend(ql, qr, pb, segs), (q_lat, q_rope), pos)
    else:
        segs = (past, (ckv, kr, pos))
        o_lat = _mla_attend(q_lat, q_rope, pos, segs)
    o = jnp.einsum('bthc,chd->bthd', o_lat, w_uv).reshape(b, t, MLA_HEADS * MLA_V)
    y = (o * jax.nn.silu(gate)) @ w_out
    return y, ckv, kr


def _sb_attend(q, q_pos, segs):
    z = jnp.concatenate([_gqa_scores(q, k) for k, _, _ in segs], axis=-1).astype(jnp.float32) * HEAD_DIM ** -0.5
    k_pos = jnp.concatenate([p for _, _, p in segs])
    strict = k_pos[None, :] < q_pos[:, None]
    log_keep = jnp.where(strict, jax.nn.log_sigmoid(-z), 0.0)
    log_w = jax.nn.log_sigmoid(z) + lax.cumsum(log_keep, axis=z.ndim - 1, reverse=True) - log_keep
    w = jnp.where(strict, jnp.exp(log_w), 0.0).astype(q.dtype)
    return sum(_gqa_mix(pw, seg[1]) for pw, seg in zip(_split_by_segments(w, segs), segs))


def _sb_group(x, pos, past, w_in, w_out):
    b, t, _ = x.shape
    g = SB_HEADS // SB_KV_HEADS
    nq, nkv = SB_HEADS * HEAD_DIM, SB_KV_HEADS * HEAD_DIM
    q, k, v, gate = jnp.split(x @ w_in, [nq, nq + nkv, nq + 2 * nkv], axis=-1)
    q = q.reshape(b, t, SB_KV_HEADS, g, HEAD_DIM)
    k = k.reshape(b, t, SB_KV_HEADS, HEAD_DIM)
    v = v.reshape(b, t, SB_KV_HEADS, HEAD_DIM)
    if past is None:
        segs = ((k, v, pos),)
        o = _sweep_query_blocks(lambda qb, pb: _sb_attend(qb, pb, segs), (q,), pos)
    else:
        segs = (past, (k, v, pos))
        o = _sb_attend(q, pos, segs)
    y = (o.reshape(b, t, nq) * jax.nn.silu(gate)) @ w_out
    return y, k, v


def _fox_attend(q, f_q, q_pos, segs):
    b, t, kvh, g, _ = q.shape
    s = jnp.concatenate([_gqa_scores(q, k) for k, _, _, _ in segs], axis=-1).astype(jnp.float32) * HEAD_DIM ** -0.5
    f_k = jnp.concatenate([f for _, _, f, _ in segs], axis=1)
    decay = (f_q.reshape(b, t, kvh, g).transpose(0, 2, 3, 1)[..., :, None]
             - f_k.reshape(b, -1, kvh, g).transpose(0, 2, 3, 1)[..., None, :])
    k_pos = jnp.concatenate([p for _, _, _, p in segs])
    causal = k_pos[None, :] <= q_pos[:, None]
    p = jax.nn.softmax(jnp.where(causal, s + decay, -jnp.inf), axis=-1).astype(q.dtype)
    return sum(_gqa_mix(pw, seg[1]) for pw, seg in zip(_split_by_segments(p, segs), segs))


def _fox_group(x, pos, past, w_in, b_f, w_out):
    b, t, _ = x.shape
    g = FOX_HEADS // FOX_KV_HEADS
    nq, nkv = FOX_HEADS * HEAD_DIM, FOX_KV_HEADS * HEAD_DIM
    q, k, v, f_pre, gate = jnp.split(x @ w_in, [nq, nq + nkv, nq + 2 * nkv, nq + 2 * nkv + FOX_HEADS], axis=-1)
    q = q.reshape(b, t, FOX_KV_HEADS, g, HEAD_DIM)
    k = k.reshape(b, t, FOX_KV_HEADS, HEAD_DIM)
    v = v.reshape(b, t, FOX_KV_HEADS, HEAD_DIM)
    logf = jax.nn.log_sigmoid((f_pre + b_f).astype(jnp.float32))
    if past is None:
        f_cum = jnp.cumsum(logf, axis=1)
        segs = ((k, v, f_cum, pos),)
        o = _sweep_query_blocks(lambda qb, fb, pb: _fox_attend(qb, fb, pb, segs), (q, f_cum), pos)
    else:
        k_past, v_past, logf_past, pos_past = past
        n_past = logf_past.shape[1]
        f_cum = jnp.cumsum(jnp.concatenate([logf_past.astype(jnp.float32), logf], axis=1), axis=1)
        segs = ((k_past, v_past, f_cum[:, :n_past], pos_past), (k, v, f_cum[:, n_past:], pos))
        o = _fox_attend(q, f_cum[:, n_past:], pos, segs)
    y = (o.reshape(b, t, nq) * jax.nn.silu(gate)) @ w_out
    return y, k, v, logf.astype(x.dtype)


def _mlstm_chunk(carry, chunk):
    c0, n0, m0 = carry
    q, k, v, ig, lf = chunk
    l = q.shape[1]
    bcum = jnp.cumsum(lf, axis=1)
    causal = jnp.tril(jnp.ones((l, l), dtype=bool))[None, :, :, None]
    d = jnp.where(causal, bcum[:, :, None, :] - bcum[:, None, :, :] + ig[:, None, :, :], -jnp.inf)
    inter = bcum + m0[:, None, :]
    m = jnp.maximum(inter, jnp.max(d, axis=2))
    w_intra = jnp.exp(d - m[:, :, None, :])
    w_inter = jnp.exp(inter - m)
    a = w_intra * jnp.einsum('bthd,bshd->btsh', q, k)
    num = jnp.einsum('btsh,bshv->bthv', a, v) + w_inter[..., None] * jnp.einsum('bthd,bhdv->bthv', q, c0)
    den = jnp.sum(a, axis=2) + w_inter * jnp.einsum('bthd,bhd->bth', q, n0)
    h = num / jnp.maximum(jnp.abs(den), jnp.exp(-m))[..., None]
    m_end = m[:, -1]
    w_end = jnp.exp(bcum[:, -1:, :] - bcum + ig - m_end[:, None, :])
    w_carry = jnp.exp(bcum[:, -1] + m0 - m_end)
    c1 = w_carry[..., None, None] * c0 + jnp.einsum('bsh,bshd,bshv->bhdv', w_end, k, v)
    n1 = w_carry[..., None] * n0 + jnp.einsum('bsh,bshd->bhd', w_end, k)
    return (c1, n1, m_end), h


def _mlstm_group(x, state, w_in, b_i, b_f, w_out):
    b, t, _ = x.shape
    hq, hv, nh = ML_HEADS * ML_QK, ML_HEADS * ML_V, ML_HEADS
    q, k, v, i_pre, f_pre, o_pre, gate = jnp.split(
        x @ w_in, [hq, 2 * hq, 2 * hq + hv, 2 * hq + hv + nh, 2 * hq + hv + 2 * nh, 2 * hq + 2 * hv + 2 * nh], axis=-1)
    f32 = jnp.float32
    q = q.reshape(b, t, nh, ML_QK).astype(f32) * ML_QK ** -0.5
    k = k.reshape(b, t, nh, ML_QK).astype(f32)
    v = v.reshape(b, t, nh, ML_V).astype(f32)
    ig = (i_pre + b_i).astype(f32)
    lf = jax.nn.log_sigmoid((f_pre + b_f).astype(f32))
    if state is None:
        nc = t // ML_CHUNK
        init = (jnp.zeros((b, nh, ML_QK, ML_V), f32), jnp.zeros((b, nh, ML_QK), f32),
                jnp.full((b, nh), -jnp.inf, f32))
        chunks = tuple(jnp.moveaxis(a.reshape((b, nc, ML_CHUNK) + a.shape[2:]), 1, 0) for a in (q, k, v, ig, lf))
        (c1, n1, m1), h = lax.scan(_mlstm_chunk, init, chunks)
        h = jnp.moveaxis(h, 0, 1).reshape(b, t, nh, ML_V)
    else:
        c0, n0, m0 = state
        (c1, n1, m1), h = _mlstm_chunk((c0.astype(f32), n0.astype(f32), m0.astype(f32)), (q, k, v, ig, lf))
    h = h * jax.nn.sigmoid(o_pre.astype(f32)).reshape(b, t, nh, ML_V)
    y = (h.reshape(b, t, hv).astype(x.dtype) * jax.nn.silu(gate)) @ w_out
    return y, c1.astype(x.dtype), n1.astype(x.dtype), m1.astype(x.dtype)


def setup_inputs(seed: int = 0) -> dict:
    key = jax.random.key(seed)
    keys = jax.random.split(key, 40)
    counter = iter(range(40))

    def nrm(shape, scale=1.0):
        return scale * jax.random.normal(keys[next(counter)], shape, jnp.float32)

    n_pages = PAST_LEN // PAGE_SIZE
    n_used = DEC_BATCH * n_pages
    n_pool = n_used + max(1, n_used // 4)
    page_table = jax.random.permutation(keys[next(counter)], n_pool)[:n_used].reshape(DEC_BATCH, n_pages).astype(jnp.int32)
    s_in = D_MODEL ** -0.5

    def w_out_init(fan_in):
        return nrm((fan_in, D_MODEL), fan_in ** -0.5 * DEEPNORM_OUT_SCALE)

    a_cols = MLA_Q_LORA + MLA_KV_LORA + MLA_ROPE + D_MODEL
    b_cols = (SB_HEADS + 2 * SB_KV_HEADS) * HEAD_DIM + D_MODEL
    c_cols = (FOX_HEADS + 2 * FOX_KV_HEADS) * HEAD_DIM + FOX_HEADS + D_MODEL
    d_cols = 2 * ML_HEADS * ML_QK + 2 * ML_HEADS * ML_V + 2 * ML_HEADS + D_MODEL
    return {
        'x_prompt': nrm((BATCH, SEQ, D_MODEL)),
        'x_sample': nrm((DEC_BATCH, DEC_SEQ, D_MODEL)),
        'page_table': page_table,
        'cache_a_ckv': nrm((n_pool, PAGE_SIZE, MLA_KV_LORA)),
        'cache_a_krope': nrm((n_pool, PAGE_SIZE, MLA_ROPE)),
        'cache_b_k': nrm((n_pool, PAGE_SIZE, SB_KV_HEADS, HEAD_DIM)),
        'cache_b_v': nrm((n_pool, PAGE_SIZE, SB_KV_HEADS, HEAD_DIM)),
        'cache_c_k': nrm((n_pool, PAGE_SIZE, FOX_KV_HEADS, HEAD_DIM)),
        'cache_c_v': nrm((n_pool, PAGE_SIZE, FOX_KV_HEADS, HEAD_DIM)),
        'cache_c_logf': jax.nn.log_sigmoid(3.0 + nrm((n_pool, PAGE_SIZE, FOX_HEADS))),
        'state_d_C': nrm((DEC_BATCH, ML_HEADS, ML_QK, ML_V)),
        'state_d_n': nrm((DEC_BATCH, ML_HEADS, ML_QK)),
        'state_d_m': 1.0 + nrm((DEC_BATCH, ML_HEADS)),
        'ln_g': 1.0 + nrm((DEPTH, D_MODEL), 0.02),
        'ln_b': nrm((DEPTH, D_MODEL), 0.02),
        'a_w_in': nrm((D_MODEL, a_cols), s_in),
        'a_g_cq': 1.0 + nrm((MLA_Q_LORA,), 0.02),
        'a_g_ckv': 1.0 + nrm((MLA_KV_LORA,), 0.02),
        'a_w_uq': nrm((MLA_Q_LORA, MLA_HEADS, MLA_NOPE + MLA_ROPE), MLA_Q_LORA ** -0.5),
        'a_w_uk': nrm((MLA_KV_LORA, MLA_HEADS, MLA_NOPE), MLA_KV_LORA ** -0.5),
        'a_w_uv': nrm((MLA_KV_LORA, MLA_HEADS, MLA_V), MLA_KV_LORA ** -0.5),
        'a_w_out': w_out_init(MLA_HEADS * MLA_V),
        'b_w_in': nrm((D_MODEL, b_cols), s_in),
        'b_w_out': w_out_init(SB_HEADS * HEAD_DIM),
        'c_w_in': nrm((D_MODEL, c_cols), s_in),
        'c_b_f': 3.0 + nrm((FOX_HEADS,), 0.1),
        'c_w_out': w_out_init(FOX_HEADS * HEAD_DIM),
        'd_w_in': nrm((D_MODEL, d_cols), s_in),
        'd_b_i': nrm((ML_HEADS,), 0.1),
        'd_b_f': 3.0 + nrm((ML_HEADS,), 0.1),
        'd_w_out': w_out_init(ML_HEADS * ML_V),
    }


def reference(x_prompt, x_sample, page_table, cache_a_ckv, cache_a_krope, cache_b_k, cache_b_v,
              cache_c_k, cache_c_v, cache_c_logf, state_d_C, state_d_n, state_d_m, ln_g, ln_b,
              a_w_in, a_g_cq, a_g_ckv, a_w_uq, a_w_uk, a_w_uv, a_w_out, b_w_in, b_w_out,
              c_w_in, c_b_f, c_w_out, d_w_in, d_b_i, d_b_f, d_w_out):
    past_len = page_table.shape[1] * PAGE_SIZE
    pos_p = jnp.arange(x_prompt.shape[1], dtype=jnp.int32)
    pos_past = jnp.arange(past_len, dtype=jnp.int32)
    pos_s = past_len + jnp.arange(x_sample.shape[1], dtype=jnp.int32)
    xp, xs = x_prompt, x_sample
    for layer in range(DEPTH):
        kind = layer % N_MIXERS
        if kind == 0:
            wa = (a_w_in, a_g_cq, a_g_ckv, a_w_uq, a_w_uk, a_w_uv, a_w_out)
            yp, a_ckv_p, a_kr_p = _mla_group(xp, pos_p, None, *wa)
            past = (_gather_pages(cache_a_ckv, page_table), _gather_pages(cache_a_krope, page_table), pos_past)
            ys, a_ckv_s, a_kr_s = _mla_group(xs, pos_s, past, *wa)
        elif kind == 1:
            yp, b_k_p, b_v_p = _sb_group(xp, pos_p, None, b_w_in, b_w_out)
            past = (_gather_pages(cache_b_k, page_table), _gather_pages(cache_b_v, page_table), pos_past)
            ys, b_k_s, b_v_s = _sb_group(xs, pos_s, past, b_w_in, b_w_out)
        elif kind == 2:
            yp, c_k_p, c_v_p, c_logf_p = _fox_group(xp, pos_p, None, c_w_in, c_b_f, c_w_out)
            past = (_gather_pages(cache_c_k, page_table), _gather_pages(cache_c_v, page_table),
                    _gather_pages(cache_c_logf, page_table), pos_past)
            ys, c_k_s, c_v_s, c_logf_s = _fox_group(xs, pos_s, past, c_w_in, c_b_f, c_w_out)
        else:
            yp, d_C_p, d_n_p, d_m_p = _mlstm_group(xp, None, d_w_in, d_b_i, d_b_f, d_w_out)
            ys, d_C_s, d_n_s, d_m_s = _mlstm_group(xs, (state_d_C, state_d_n, state_d_m), d_w_in, d_b_i, d_b_f, d_w_out)
        xp = _layernorm(DEEPNORM_ALPHA * xp + yp, ln_g[layer], ln_b[layer])
        xs = _layernorm(DEEPNORM_ALPHA * xs + ys, ln_g[layer], ln_b[layer])
    return (xp, xs, a_ckv_p, a_kr_p, a_ckv_s, a_kr_s, b_k_p, b_v_p, b_k_s, b_v_s,
            c_k_p, c_v_p, c_logf_p, c_k_s, c_v_s, c_logf_s, d_C_p, d_n_p, d_m_p, d_C_s, d_n_s, d_m_s)
```

```python
import functools

import jax
import jax.numpy as jnp
import numpy as np
from jax import lax
from jax.experimental import pallas as pl
from jax.experimental.pallas import tpu as pltpu

F32 = jnp.float32
BF16 = jnp.bfloat16

D_MODEL = 2048
HEAD_DIM = 128
N_HEADS = D_MODEL // HEAD_DIM
KV_HEADS = N_HEADS // 4
GROUP = N_HEADS // KV_HEADS
KV_WIDTH = KV_HEADS * HEAD_DIM
PAGE = 128
DEPTH = 4
ALPHA = (2 * DEPTH) ** 0.25
LN_EPS = 1e-5
RMS_EPS = 1e-6
ROPE_THETA = 10000.0
LORA = D_MODEL // 4
ROPE = HEAD_DIM // 2
QK_PAD = 2 * HEAD_DIM
ML_HEADS = 4
ML_V = D_MODEL // ML_HEADS
ML_QK = ML_V // 2
LANE = 128
VMEM_LIMIT = 56 * 1024 * 1024
NEG_INF = float("-inf")


def _cparams(n_axes):
    return pltpu.CompilerParams(dimension_semantics=("arbitrary",) * n_axes,
                                vmem_limit_bytes=VMEM_LIMIT)


def _dot(a, b):
    return jnp.dot(a, b, preferred_element_type=F32)


def _dot_nt(a, b):
    return lax.dot_general(a, b, (((1,), (1,)), ((), ())), preferred_element_type=F32)


def _log_sigmoid(z):
    return jnp.minimum(z, 0.0) - jnp.log(1.0 + jnp.exp(-jnp.abs(z)))


def _sigmoid(z):
    return 1.0 / (1.0 + jnp.exp(-z))


def _split_bf16(x):
    hi = x.astype(BF16)
    lo = (x - hi.astype(F32)).astype(BF16)
    return hi, lo


def _tri(n, kind):
    r = lax.broadcasted_iota(jnp.int32, (n, n), 0)
    c = lax.broadcasted_iota(jnp.int32, (n, n), 1)
    if kind == "lower_incl":
        m = c <= r
    else:
        m = r > c
    return jnp.where(m, 1.0, 0.0).astype(BF16)


def _pair_tables(n, descending):
    qs, ks = [], []
    for qi in range(n):
        order = range(qi, -1, -1) if descending else range(qi + 1)
        for kj in order:
            qs.append(qi)
            ks.append(kj)
    return jnp.asarray(np.array(qs, np.int32)), jnp.asarray(np.array(ks, np.int32))


def _mm_kernel(x_ref, w_ref, o_ref):
    o_ref[...] = _dot(x_ref[...], w_ref[...])


def _matmul(xb, wb, tm, tn):
    m, k = xb.shape
    n = wb.shape[1]
    return pl.pallas_call(
        _mm_kernel, grid=(m // tm, n // tn),
        in_specs=[pl.BlockSpec((tm, k), lambda i, j: (i, 0)),
                  pl.BlockSpec((k, tn), lambda i, j: (0, j))],
        out_specs=pl.BlockSpec((tm, tn), lambda i, j: (i, j)),
        out_shape=jax.ShapeDtypeStruct((m, n), F32),
        compiler_params=_cparams(2), name="proj")(xb, wb)


def _proj_tiles(m, n):
    tm = max(c for c in (1024, 512, 256) if m % c == 0)
    units = n // 256
    d = max(c for c in (1, 2, 3, 4) if units % c == 0)
    return tm, 256 * d


def _out_ln_kernel(o_ref, gate_ref, x_ref, w_ref, g_ref, b_ref, xo_ref, xb_ref):
    g = gate_ref[...]
    a = (o_ref[...] * (g * _sigmoid(g))).astype(BF16)
    z = ALPHA * x_ref[...] + _dot(a, w_ref[...])
    mu = jnp.mean(z, axis=-1, keepdims=True)
    zc = z - mu
    var = jnp.mean(zc * zc, axis=-1, keepdims=True)
    out = zc * lax.rsqrt(var + LN_EPS) * g_ref[...] + b_ref[...]
    xo_ref[...] = out
    xb_ref[...] = out.astype(BF16)


def _out_ln(o, proj, x, wb, ln_g, ln_b):
    m = x.shape[0]
    tm = 256
    row = lambda i: (i, 0)
    fixed = lambda i: (0, 0)
    return pl.pallas_call(
        _out_ln_kernel, grid=(m // tm,),
        in_specs=[pl.BlockSpec((tm, D_MODEL), row), pl.BlockSpec((tm, D_MODEL), row),
                  pl.BlockSpec((tm, D_MODEL), row), pl.BlockSpec((D_MODEL, D_MODEL), fixed),
                  pl.BlockSpec((1, D_MODEL), fixed), pl.BlockSpec((1, D_MODEL), fixed)],
        out_specs=[pl.BlockSpec((tm, D_MODEL), row), pl.BlockSpec((tm, D_MODEL), row)],
        out_shape=[jax.ShapeDtypeStruct((m, D_MODEL), F32), jax.ShapeDtypeStruct((m, D_MODEL), BF16)],
        compiler_params=_cparams(1), name="out_ln")(o, proj, x, wb, ln_g, ln_b)


def _rms(x, g):
    return x * lax.rsqrt(jnp.mean(x * x, axis=-1, keepdims=True) + RMS_EPS) * g


def _mla_mid_kernel(cq_ref, ckv_ref, kr_ref, krp_ref, gq_ref, gkv_ref, cos_ref, sin_ref, wq_ref, *rest,
                    with_kv):
    if with_kv:
        wkv_ref, q_ref, ckvn_ref, krr_ref, k_ref, v_ref = rest
    else:
        q_ref, ckvn_ref, krr_ref = rest
    cos = cos_ref[...]
    sin = sin_ref[...]
    cqn = _rms(cq_ref[...], gq_ref[...]).astype(BF16)
    qa = _dot(cqn, wq_ref[...])
    krr = kr_ref[...] * cos + krp_ref[...] * sin
    krr_ref[...] = krr
    ckvn = _rms(ckv_ref[...], gkv_ref[...])
    ckvn_ref[...] = ckvn
    for h in range(N_HEADS):
        lo = h * HEAD_DIM
        q_ref[:, h * QK_PAD:h * QK_PAD + HEAD_DIM] = qa[:, lo:lo + HEAD_DIM].astype(BF16)
        qr = qa[:, D_MODEL + lo:D_MODEL + lo + HEAD_DIM] * cos + qa[:, 2 * D_MODEL + lo:2 * D_MODEL + lo + HEAD_DIM] * sin
        q_ref[:, h * QK_PAD + HEAD_DIM:(h + 1) * QK_PAD] = qr.astype(BF16)
    if with_kv:
        kv = _dot(ckvn.astype(BF16), wkv_ref[...])
        krb = krr.astype(BF16)
        for h in range(N_HEADS):
            lo = h * HEAD_DIM
            k_ref[:, h * QK_PAD:h * QK_PAD + HEAD_DIM] = kv[:, lo:lo + HEAD_DIM].astype(BF16)
            k_ref[:, h * QK_PAD + HEAD_DIM:(h + 1) * QK_PAD] = krb
        v_ref[...] = kv[:, D_MODEL:].astype(BF16)


def _mla_mid(proj, g_cq, g_ckv, cos_t, sin_t, wq, wkv):
    m = proj.shape[0]
    tm = 256
    with_kv = wkv is not None
    fixed = lambda i: (0, 0)
    in_specs = [pl.BlockSpec((tm, LORA), lambda i: (i, D_MODEL // LORA)),
                pl.BlockSpec((tm, LORA), lambda i: (i, D_MODEL // LORA + 1)),
                pl.BlockSpec((tm, LANE), lambda i: (i, (D_MODEL + 2 * LORA) // LANE)),
                pl.BlockSpec((tm, LANE), lambda i: (i, (D_MODEL + 2 * LORA) // LANE + 1)),
                pl.BlockSpec((1, LORA), fixed), pl.BlockSpec((1, LORA), fixed),
                pl.BlockSpec((tm, LANE), lambda i: (i, 0)), pl.BlockSpec((tm, LANE), lambda i: (i, 0)),
                pl.BlockSpec((LORA, 3 * D_MODEL), fixed)]
    args = [proj, proj, proj, proj, g_cq, g_ckv, cos_t, sin_t, wq]
    row = lambda i: (i, 0)
    out_specs = [pl.BlockSpec((tm, N_HEADS * QK_PAD), row), pl.BlockSpec((tm, LORA), row),
                 pl.BlockSpec((tm, LANE), row)]
    out_shape = [jax.ShapeDtypeStruct((m, N_HEADS * QK_PAD), BF16), jax.ShapeDtypeStruct((m, LORA), F32),
                 jax.ShapeDtypeStruct((m, LANE), F32)]
    if with_kv:
        in_specs.append(pl.BlockSpec((LORA, 2 * D_MODEL), fixed))
        args.append(wkv)
        out_specs += [pl.BlockSpec((tm, N_HEADS * QK_PAD), row), pl.BlockSpec((tm, D_MODEL), row)]
        out_shape += [jax.ShapeDtypeStruct((m, N_HEADS * QK_PAD), BF16), jax.ShapeDtypeStruct((m, D_MODEL), BF16)]
    return pl.pallas_call(
        functools.partial(_mla_mid_kernel, with_kv=with_kv), grid=(m // tm,),
        in_specs=in_specs, out_specs=out_specs, out_shape=out_shape,
        compiler_params=_cparams(1), name="mla_mid")(*args)


def _head_mm_kernel(x_ref, w_ref, o_ref):
    o_ref[...] = _dot(x_ref[...].astype(BF16), w_ref[...]).astype(o_ref.dtype)


def _head_matmul(x, w, *, x_width, x_stride, w_rows, out_width, out_dtype):
    m = x.shape[0]
    if w_rows:
        w_spec = pl.BlockSpec((x_width, out_width), lambda h: (h, 0))
    else:
        w_spec = pl.BlockSpec((x_width, out_width), lambda h: (0, h))
    return pl.pallas_call(
        _head_mm_kernel, grid=(N_HEADS,),
        in_specs=[pl.BlockSpec((m, x_width), lambda h: (0, h * x_stride)), w_spec],
        out_specs=pl.BlockSpec((m, out_width), lambda h: (0, h)),
        out_shape=jax.ShapeDtypeStruct((m, N_HEADS * out_width), out_dtype),
        compiler_params=_cparams(1), name="head_mm")(x, w)


def _flash_kernel(qt_ref, kt_ref, q_ref, k_ref, v_ref, *rest, scale, tq, has_bias):
    if has_bias:
        fq_ref, fk_ref, o_ref, m_s, l_s, acc_s = rest
    else:
        o_ref, m_s, l_s, acc_s = rest
    p = pl.program_id(2)
    qi = qt_ref[p]
    kj = kt_ref[p]

    @pl.when(kj == 0)
    def _():
        m_s[...] = jnp.full(m_s.shape, NEG_INF, F32)
        l_s[...] = jnp.zeros(l_s.shape, F32)
        acc_s[...] = jnp.zeros(acc_s.shape, F32)

    s = _dot_nt(q_ref[...].astype(BF16), k_ref[...].astype(BF16)) * scale
    if has_bias:
        s = s + (fq_ref[0, 0] - fk_ref[0, 0])
    row = qi * tq + lax.broadcasted_iota(jnp.int32, s.shape, 0)
    col = kj * tq + lax.broadcasted_iota(jnp.int32, s.shape, 1)
    s = jnp.where(col <= row, s, NEG_INF)
    m_old = m_s[...]
    m_new = jnp.maximum(m_old, jnp.max(s, axis=1, keepdims=True))
    alpha = jnp.exp(m_old - m_new)
    pe = jnp.exp(s - m_new)
    l_s[...] = alpha * l_s[...] + jnp.sum(pe, axis=1, keepdims=True)
    acc_s[...] = alpha * acc_s[...] + _dot(pe.astype(BF16), v_ref[...].astype(BF16))
    m_s[...] = m_new

    @pl.when(kj == qi)
    def _():
        o_ref[...] = acc_s[...] / l_s[...]


def _flash_prompt(q, k, v, *, batch, seq, group, dqk, dv, q_cb, k_cb, v_cb, scale, fq=None, fk=None, tq=512):
    tq = min(tq, seq)
    nq = seq // tq
    qt, kt = _pair_tables(nq, descending=False)
    has_bias = fq is not None
    in_specs = [pl.BlockSpec((tq, dqk), lambda b, h, p, qt, kt: (b * nq + qt[p], q_cb + h)),
                pl.BlockSpec((tq, dqk), lambda b, h, p, qt, kt: (b * nq + kt[p], k_cb + h // group)),
                pl.BlockSpec((tq, dv), lambda b, h, p, qt, kt: (b * nq + kt[p], v_cb + h // group))]
    args = [q, k, v]
    if has_bias:
        in_specs += [pl.BlockSpec((1, 1, tq, 1), lambda b, h, p, qt, kt: (b, h, qt[p], 0)),
                     pl.BlockSpec((1, 1, 1, tq), lambda b, h, p, qt, kt: (b, h, 0, kt[p]))]
        args += [fq, fk]
    grid_spec = pltpu.PrefetchScalarGridSpec(
        num_scalar_prefetch=2, grid=(batch, N_HEADS, int(qt.shape[0])), in_specs=in_specs,
        out_specs=pl.BlockSpec((tq, dv), lambda b, h, p, qt, kt: (b * nq + qt[p], h)),
        scratch_shapes=[pltpu.VMEM((tq, 1), F32), pltpu.VMEM((tq, 1), F32), pltpu.VMEM((tq, dv), F32)])
    return pl.pallas_call(
        functools.partial(_flash_kernel, scale=scale, tq=tq, has_bias=has_bias), grid_spec=grid_spec,
        out_shape=jax.ShapeDtypeStruct((batch * seq, N_HEADS * dv), F32),
        compiler_params=_cparams(3), name="flash_prompt")(qt, kt, *args)


def _sb_kernel(qt_ref, kt_ref, q_ref, k_ref, v_ref, o_ref, qs_s, acc_s, car_s, *, tq, scale):
    p = pl.program_id(2)
    qi = qt_ref[p]
    kj = kt_ref[p]

    @pl.when(kj == qi)
    def _():
        for g in range(GROUP):
            qs_s[g * tq:(g + 1) * tq, :] = q_ref[:, g * HEAD_DIM:(g + 1) * HEAD_DIM].astype(BF16)
        acc_s[...] = jnp.zeros(acc_s.shape, F32)
        car_s[...] = jnp.zeros(car_s.shape, F32)

    z = _dot_nt(qs_s[...], k_ref[...].astype(BF16)) * scale
    qpos = qi * tq + (lax.broadcasted_iota(jnp.int32, z.shape, 0) & (tq - 1))
    kpos = kj * tq + lax.broadcasted_iota(jnp.int32, z.shape, 1)
    strict = kpos < qpos
    ls = _log_sigmoid(z)
    lk = jnp.where(strict, ls - z, 0.0)
    hi, lo = _split_bf16(lk)
    later = _tri(tq, "later")
    sfx = _dot(hi, later) + _dot(lo, later)
    w = jnp.where(strict, jnp.exp(ls + sfx + car_s[...]), 0.0).astype(BF16)
    acc_s[...] += _dot(w, v_ref[...].astype(BF16))
    car_s[...] += (sfx + lk)[:, 0:1]

    @pl.when(kj == 0)
    def _():
        for g in range(GROUP):
            o_ref[:, g * HEAD_DIM:(g + 1) * HEAD_DIM] = acc_s[g * tq:(g + 1) * tq, :]


def _sb_prompt(proj, *, batch, seq, q_cb, k_cb, v_cb, tq=256):
    tq = min(tq, seq)
    nq = seq // tq
    qt, kt = _pair_tables(nq, descending=True)
    grid_spec = pltpu.PrefetchScalarGridSpec(
        num_scalar_prefetch=2, grid=(batch, KV_HEADS, int(qt.shape[0])),
        in_specs=[pl.BlockSpec((tq, KV_WIDTH), lambda b, h, p, qt, kt: (b * nq + qt[p], q_cb + h)),
                  pl.BlockSpec((tq, HEAD_DIM), lambda b, h, p, qt, kt: (b * nq + kt[p], k_cb + h)),
                  pl.BlockSpec((tq, HEAD_DIM), lambda b, h, p, qt, kt: (b * nq + kt[p], v_cb + h))],
        out_specs=pl.BlockSpec((tq, KV_WIDTH), lambda b, h, p, qt, kt: (b * nq + qt[p], h)),
        scratch_shapes=[pltpu.VMEM((GROUP * tq, HEAD_DIM), BF16), pltpu.VMEM((GROUP * tq, HEAD_DIM), F32),
                        pltpu.VMEM((GROUP * tq, 1), F32)])
    return pl.pallas_call(
        functools.partial(_sb_kernel, tq=tq, scale=HEAD_DIM ** -0.5), grid_spec=grid_spec,
        out_shape=jax.ShapeDtypeStruct((batch * seq, D_MODEL), F32),
        compiler_params=_cparams(3), name="sb_prompt")(qt, kt, proj, proj, proj)


def _fox_gates_kernel(f_ref, bf_ref, lf_ref, fc_ref, carry_s, *, tb):
    @pl.when(pl.program_id(1) == 0)
    def _():
        carry_s[...] = jnp.zeros(carry_s.shape, F32)

    lf = _log_sigmoid(f_ref[...] + bf_ref[...])
    hi, lo = _split_bf16(lf)
    tri = _tri(tb, "lower_incl")
    cum = _dot(tri, hi) + _dot(tri, lo) + carry_s[...]
    lf_ref[...] = lf
    fc_ref[...] = cum
    carry_s[...] = cum[tb - 1:tb, :]


def _fox_gates(proj, bf_pad, *, batch, seq, f_cb, tb=512):
    tb = min(tb, seq)
    nb = seq // tb
    spec = pl.BlockSpec((tb, LANE), lambda b, t: (b * nb + t, 0))
    return pl.pallas_call(
        functools.partial(_fox_gates_kernel, tb=tb), grid=(batch, nb),
        in_specs=[pl.BlockSpec((tb, LANE), lambda b, t: (b * nb + t, f_cb)), pl.BlockSpec((1, LANE), lambda b, t: (0, 0))],
        out_specs=[spec, spec],
        out_shape=[jax.ShapeDtypeStruct((batch * seq, LANE), F32)] * 2,
        scratch_shapes=[pltpu.VMEM((1, LANE), F32)],
        compiler_params=_cparams(2), name="fox_gates")(proj, bf_pad)


def _mlstm_kernel(bi_ref, bf_ref, q_ref, k_ref, v_ref, op_ref, ir_ref, fr_ref, ic_ref, fc_ref,
                  c0_ref, n0_ref, m0_ref, h_ref, c1_ref, n1_ref, m1_ref, c_s, n_s, m_s, *pad_s,
                  chunk, rows_in, n_chunks):
    head = pl.program_id(1)
    ci = pl.program_id(2)

    @pl.when(ci == 0)
    def _():
        c_s[...] = c0_ref[0, 0]
        n_s[...] = n0_ref[0, 0]
        m_s[...] = m0_ref[0, 0]

    b_i = bi_ref[head]
    b_f = bf_ref[head]
    if rows_in == chunk:
        q, k, v = q_ref[0], k_ref[0], v_ref[0]
        i_row, f_row = ir_ref[0, 0], fr_ref[0, 0]
        i_col, f_col = ic_ref[0, 0], fc_ref[0, 0]
    else:
        q_p, k_p, v_p, ir_p, fr_p, ic_p, fc_p = pad_s
        for buf, src in ((q_p, q_ref[0]), (k_p, k_ref[0]), (v_p, v_ref[0])):
            buf[...] = jnp.zeros(buf.shape, F32)
            buf[0:rows_in, :] = src
        for buf, src in ((ir_p, ir_ref[0, 0]), (fr_p, fr_ref[0, 0])):
            buf[...] = jnp.zeros(buf.shape, F32)
            buf[:, 0:rows_in] = src
        for buf, src in ((ic_p, ic_ref[0, 0]), (fc_p, fc_ref[0, 0])):
            buf[...] = jnp.zeros(buf.shape, F32)
            buf[0:rows_in, :] = src
        q, k, v = q_p[...], k_p[...], v_p[...]
        i_row, f_row, i_col, f_col = ir_p[...], fr_p[...], ic_p[...], fc_p[...]
    ig_row = i_row + b_i
    lf_row = _log_sigmoid(f_row + b_f)
    ig_col = i_col + b_i
    lf_col = _log_sigmoid(f_col + b_f)
    if rows_in != chunk:
        live_row = lax.broadcasted_iota(jnp.int32, (1, chunk), 1) < rows_in
        live_col = lax.broadcasted_iota(jnp.int32, (chunk, 1), 0) < rows_in
        ig_row = jnp.where(live_row, ig_row, NEG_INF)
        lf_row = jnp.where(live_row, lf_row, 0.0)
        ig_col = jnp.where(live_col, ig_col, NEG_INF)
        lf_col = jnp.where(live_col, lf_col, 0.0)
    t_i = lax.broadcasted_iota(jnp.int32, (chunk, chunk), 0)
    s_i = lax.broadcasted_iota(jnp.int32, (chunk, chunk), 1)
    causal = s_i <= t_i
    bcum_col = jnp.sum(jnp.where(causal, lf_row, 0.0), axis=1, keepdims=True)
    bcum_row = jnp.sum(jnp.where(t_i <= s_i, lf_col, 0.0), axis=0, keepdims=True)
    d = jnp.where(causal, bcum_col - bcum_row + ig_row, NEG_INF)
    m0 = m_s[...]
    inter = bcum_col + m0
    m = jnp.maximum(inter, jnp.max(d, axis=1, keepdims=True))
    w_intra = jnp.exp(d - m)
    w_inter = jnp.exp(inter - m)
    qs = q * (ML_QK ** -0.5)
    qb = qs.astype(BF16)
    vb = v.astype(BF16)
    a = w_intra * _dot_nt(qb, k.astype(BF16))
    c_old = c_s[...]
    n_old = n_s[...]
    num = _dot(a.astype(BF16), vb) + w_inter * _dot(qb, c_old.astype(BF16))
    den = jnp.sum(a, axis=1, keepdims=True) + w_inter * jnp.sum(qs * n_old, axis=1, keepdims=True)
    hval = num / jnp.maximum(jnp.abs(den), jnp.exp(-m))
    h_ref[0] = hval[0:rows_in] * _sigmoid(op_ref[0])
    m_end = m[chunk - 1:chunk, :]
    b_last = bcum_col[chunk - 1:chunk, :]
    w_end = jnp.exp(b_last - bcum_col + ig_col - m_end)
    w_carry = jnp.exp(b_last + m0 - m_end)
    kw = k * w_end
    c_new = w_carry * c_old + _dot(jnp.transpose(kw).astype(BF16), vb)
    n_new = w_carry * n_old + jnp.sum(kw, axis=0, keepdims=True)
    c_s[...] = c_new
    n_s[...] = n_new
    m_s[...] = m_end

    @pl.when(ci == n_chunks - 1)
    def _():
        c1_ref[0, 0] = c_new
        n1_ref[0, 0] = n_new
        m1_ref[0, 0] = m_end


def _mlstm(proj3, gates, b_i, b_f, c0, n0, m0, *, batch, n_chunks, chunk, q_cb, k_cb, v_cb, op_cb):
    rows_in = proj3.shape[1]
    i_row, f_row, i_col, f_col = gates
    tok = lambda col: (lambda b, h, c, bi, bf: (b * n_chunks + c, 0, col + h))
    st = lambda b, h, c, bi, bf: (b, h, 0, 0)
    in_specs = [pl.BlockSpec((1, rows_in, ML_QK), tok(q_cb)), pl.BlockSpec((1, rows_in, ML_QK), tok(k_cb)),
                pl.BlockSpec((1, rows_in, ML_V), tok(v_cb)), pl.BlockSpec((1, rows_in, ML_V), tok(op_cb)),
                pl.BlockSpec((1, 1, 1, rows_in), lambda b, h, c, bi, bf: (b, h, 0, c)),
                pl.BlockSpec((1, 1, 1, rows_in), lambda b, h, c, bi, bf: (b, h, 0, c)),
                pl.BlockSpec((1, 1, rows_in, 1), lambda b, h, c, bi, bf: (b, h, c, 0)),
                pl.BlockSpec((1, 1, rows_in, 1), lambda b, h, c, bi, bf: (b, h, c, 0)),
                pl.BlockSpec((1, 1, ML_QK, ML_V), st), pl.BlockSpec((1, 1, 1, ML_QK), st),
                pl.BlockSpec((1, 1, 1, 1), st)]
    out_specs = [pl.BlockSpec((1, rows_in, ML_V), tok(0)),
                 pl.BlockSpec((1, 1, ML_QK, ML_V), st), pl.BlockSpec((1, 1, 1, ML_QK), st),
                 pl.BlockSpec((1, 1, 1, 1), st)]
    out_shape = [jax.ShapeDtypeStruct((proj3.shape[0], rows_in, D_MODEL), F32),
                 jax.ShapeDtypeStruct((batch, ML_HEADS, ML_QK, ML_V), F32),
                 jax.ShapeDtypeStruct((batch, ML_HEADS, 1, ML_QK), F32),
                 jax.ShapeDtypeStruct((batch, ML_HEADS, 1, 1), F32)]
    scratch = [pltpu.VMEM((ML_QK, ML_V), F32), pltpu.VMEM((1, ML_QK), F32), pltpu.VMEM((1, 1), F32)]
    if rows_in != chunk:
        scratch += [pltpu.VMEM((chunk, ML_QK), F32), pltpu.VMEM((chunk, ML_QK), F32), pltpu.VMEM((chunk, ML_V), F32),
                    pltpu.VMEM((1, chunk), F32), pltpu.VMEM((1, chunk), F32),
                    pltpu.VMEM((chunk, 1), F32), pltpu.VMEM((chunk, 1), F32)]
    grid_spec = pltpu.PrefetchScalarGridSpec(
        num_scalar_prefetch=2, grid=(batch, ML_HEADS, n_chunks), in_specs=in_specs, out_specs=out_specs,
        scratch_shapes=scratch)
    return pl.pallas_call(
        functools.partial(_mlstm_kernel, chunk=chunk, rows_in=rows_in, n_chunks=n_chunks), grid_spec=grid_spec,
        out_shape=out_shape, compiler_params=_cparams(3), name="mlstm")(
            b_i, b_f, proj3, proj3, proj3, proj3, i_row, f_row, i_col, f_col, c0, n0, m0)


def _page_spec(width, j, n_pg, n_pages, reverse):
    def imap(b, s, pt):
        pg = n_pages - 1 - (s * n_pg + j) if reverse else s * n_pg + j
        return (pt[b * n_pages + pg], 0, 0)
    return pl.BlockSpec((1, PAGE, width), imap)


def _softmax_update(s, v_list, m_s, l_s, acc_s):
    m_old = m_s[...]
    m_new = jnp.maximum(m_old, jnp.max(s, axis=1, keepdims=True))
    alpha = jnp.exp(m_old - m_new)
    pe = jnp.exp(s - m_new)
    l_s[...] = alpha * l_s[...] + jnp.sum(pe, axis=1, keepdims=True)
    acc = alpha * acc_s[...]
    for j, vb in enumerate(v_list):
        acc = acc + _dot(pe[:, j * PAGE:(j + 1) * PAGE].astype(BF16), vb)
    acc_s[...] = acc
    m_s[...] = m_new


def _softmax_init(m_s, l_s, acc_s):
    m_s[...] = jnp.full(m_s.shape, NEG_INF, F32)
    l_s[...] = jnp.zeros(l_s.shape, F32)
    acc_s[...] = jnp.zeros(acc_s.shape, F32)


def _kv_head_select(acc):
    rows = acc.shape[0]
    kv_of_row = (lax.broadcasted_iota(jnp.int32, (rows, HEAD_DIM), 0) % N_HEADS) // GROUP
    out = jnp.zeros((rows, HEAD_DIM), F32)
    for kv in range(KV_HEADS):
        out = jnp.where(kv_of_row == kv, acc[:, kv * HEAD_DIM:(kv + 1) * HEAD_DIM], out)
    return out


def _mla_dec_kernel(pt_ref, ql_ref, qr_ref, cn_ref, kn_ref, *rest, n_pg, scale):
    c_refs, r_refs = rest[:n_pg], rest[n_pg:2 * n_pg]
    o_ref, m_s, l_s, acc_s, cb_s, kr_s = rest[2 * n_pg:]
    step = pl.program_id(1)
    ql = ql_ref[0]
    qr = qr_ref[0]

    @pl.when(step == 0)
    def _():
        _softmax_init(m_s, l_s, acc_s)
        kr_s[...] = jnp.zeros(kr_s.shape, F32)
        cn = cn_ref[0]
        s = (_dot_nt(ql, cn) + _dot_nt(qr, kn_ref[0])) * scale
        tok = lax.broadcasted_iota(jnp.int32, s.shape, 0) // N_HEADS
        key = lax.broadcasted_iota(jnp.int32, s.shape, 1)
        _softmax_update(jnp.where(key <= tok, s, NEG_INF), [cn], m_s, l_s, acc_s)

    s_list = []
    for j in range(n_pg):
        cb_s[j] = c_refs[j][0].astype(BF16)
        kr_s[j, :, 0:ROPE] = r_refs[j][0]
        s_list.append((_dot_nt(ql, cb_s[j]) + _dot_nt(qr, kr_s[j].astype(BF16))) * scale)
    _softmax_update(jnp.concatenate(s_list, axis=1), [cb_s[j] for j in range(n_pg)], m_s, l_s, acc_s)

    @pl.when(step == pl.num_programs(1) - 1)
    def _():
        o_ref[0] = acc_s[...] / l_s[...]


def _mla_decode(pt_flat, ql, qr, cn, kn, cache_ckv, cache_kr, n_pages, n_pg):
    nb, rows, _ = ql.shape
    per_b = lambda b, s, pt: (b, 0, 0)
    in_specs = [pl.BlockSpec((1, rows, LORA), per_b), pl.BlockSpec((1, rows, LANE), per_b),
                pl.BlockSpec((1, PAGE, LORA), per_b), pl.BlockSpec((1, PAGE, LANE), per_b)]
    in_specs += [_page_spec(LORA, j, n_pg, n_pages, False) for j in range(n_pg)]
    in_specs += [_page_spec(ROPE, j, n_pg, n_pages, False) for j in range(n_pg)]
    grid_spec = pltpu.PrefetchScalarGridSpec(
        num_scalar_prefetch=1, grid=(nb, n_pages // n_pg), in_specs=in_specs,
        out_specs=pl.BlockSpec((1, rows, LORA), per_b),
        scratch_shapes=[pltpu.VMEM((rows, 1), F32), pltpu.VMEM((rows, 1), F32), pltpu.VMEM((rows, LORA), F32),
                        pltpu.VMEM((n_pg, PAGE, LORA), BF16), pltpu.VMEM((n_pg, PAGE, LANE), F32)])
    return pl.pallas_call(
        functools.partial(_mla_dec_kernel, n_pg=n_pg, scale=(HEAD_DIM + ROPE) ** -0.5), grid_spec=grid_spec,
        out_shape=jax.ShapeDtypeStruct((nb, rows, LORA), F32),
        compiler_params=_cparams(2), name="mla_decode")(
            pt_flat, ql, qr, cn, kn, *([cache_ckv] * n_pg), *([cache_kr] * n_pg))


def _sb_dec_kernel(pt_ref, q_ref, kn_ref, vn_ref, *rest, n_pg, scale):
    k_refs, v_refs = rest[:n_pg], rest[n_pg:2 * n_pg]
    o_ref, acc_s, car_s = rest[2 * n_pg:]
    step = pl.program_id(1)
    qb = q_ref[0]
    rows = qb.shape[0]
    later = _tri(PAGE, "later")

    @pl.when(step == 0)
    def _():
        z = _dot_nt(qb, kn_ref[0]) * scale
        tok = lax.broadcasted_iota(jnp.int32, z.shape, 0) // N_HEADS
        key = lax.broadcasted_iota(jnp.int32, z.shape, 1)
        strict = key < tok
        ls = _log_sigmoid(z)
        lk = jnp.where(strict, ls - z, 0.0)
        hi, lo = _split_bf16(lk)
        sfx = _dot(hi, later) + _dot(lo, later)
        w = jnp.where(strict, jnp.exp(ls + sfx), 0.0).astype(BF16)
        acc_s[...] = _dot(w, vn_ref[0])
        car_s[...] = (sfx + lk)[:, 0:1]

    ls_list, lk_list = [], []
    for j in range(n_pg):
        z = _dot_nt(qb, k_refs[j][0].astype(BF16)) * scale
        ls = _log_sigmoid(z)
        ls_list.append(ls)
        lk_list.append(ls - z)
    hi, lo = _split_bf16(jnp.concatenate(lk_list, axis=0))
    sfx_all = _dot(hi, later) + _dot(lo, later)
    car = car_s[...]
    acc = acc_s[...]
    for j in range(n_pg):
        sfx = sfx_all[j * rows:(j + 1) * rows]
        w = jnp.exp(ls_list[j] + sfx + car).astype(BF16)
        acc = acc + _dot(w, v_refs[j][0].astype(BF16))
        car = car + (sfx + lk_list[j])[:, 0:1]
    acc_s[...] = acc
    car_s[...] = car

    @pl.when(step == pl.num_programs(1) - 1)
    def _():
        o_ref[0] = _kv_head_select(acc)


def _sb_decode(pt_flat, qbd, kn, vn, cache_k, cache_v, n_pages, n_pg):
    nb, rows, _ = qbd.shape
    per_b = lambda b, s, pt: (b, 0, 0)
    in_specs = [pl.BlockSpec((1, rows, KV_WIDTH), per_b), pl.BlockSpec((1, PAGE, KV_WIDTH), per_b),
                pl.BlockSpec((1, PAGE, KV_WIDTH), per_b)]
    in_specs += [_page_spec(KV_WIDTH, j, n_pg, n_pages, True) for j in range(n_pg)] * 2
    grid_spec = pltpu.PrefetchScalarGridSpec(
        num_scalar_prefetch=1, grid=(nb, n_pages // n_pg), in_specs=in_specs,
        out_specs=pl.BlockSpec((1, rows, HEAD_DIM), per_b),
        scratch_shapes=[pltpu.VMEM((rows, KV_WIDTH), F32), pltpu.VMEM((rows, 1), F32)])
    return pl.pallas_call(
        functools.partial(_sb_dec_kernel, n_pg=n_pg, scale=HEAD_DIM ** -0.5), grid_spec=grid_spec,
        out_shape=jax.ShapeDtypeStruct((nb, rows, HEAD_DIM), F32),
        compiler_params=_cparams(2), name="sb_decode")(
            pt_flat, qbd, kn, vn, *([cache_k] * n_pg), *([cache_v] * n_pg))


def _fox_dec_kernel(pt_ref, q_ref, kn_ref, vn_ref, lfn_ref, *rest, n_pg, n_new, scale):
    k_refs, v_refs, f_refs = rest[:n_pg], rest[n_pg:2 * n_pg], rest[2 * n_pg:3 * n_pg]
    o_ref, m_s, l_s, acc_s, car_s, lf_s = rest[3 * n_pg:]
    step = pl.program_id(1)
    qb = q_ref[0]
    rows = qb.shape[0]
    reps = rows // N_HEADS

    @pl.when(step == 0)
    def _():
        _softmax_init(m_s, l_s, acc_s)
        lf_s[...] = jnp.zeros(lf_s.shape, F32)
        lfn = lfn_ref[0]
        tok = lax.broadcasted_iota(jnp.int32, lfn.shape, 0) // N_HEADS
        key = lax.broadcasted_iota(jnp.int32, lfn.shape, 1)
        car_s[...] = jnp.sum(jnp.where(key <= tok, lfn, 0.0), axis=1, keepdims=True)
        dec = jnp.zeros(lfn.shape, F32)
        for jp in range(n_new):
            col = jnp.sum(jnp.where((key > jp) & (key <= tok), lfn, 0.0), axis=1, keepdims=True)
            dec = jnp.where(key == jp, col, dec)
        s = _dot_nt(qb, kn_ref[0]) * scale + dec
        _softmax_update(jnp.where(key <= tok, s, NEG_INF), [vn_ref[0]], m_s, l_s, acc_s)

    for j in range(n_pg):
        lf_s[:, j * N_HEADS:(j + 1) * N_HEADS] = f_refs[j][0]
    lft = jnp.transpose(lf_s[...])
    hi, lo = _split_bf16(lft)
    later = _tri(PAGE, "later")
    sfx = _dot(hi, later) + _dot(lo, later)
    tot = (sfx + lft)[:, 0:1]
    car = car_s[...]
    s_list, v_list = [], []
    for j in range(n_pg):
        sfx_j = jnp.concatenate([sfx[j * N_HEADS:(j + 1) * N_HEADS]] * reps, axis=0)
        z = _dot_nt(qb, k_refs[j][0].astype(BF16)) * scale
        s_list.append(z + (sfx_j + car))
        v_list.append(v_refs[j][0].astype(BF16))
        car = car + jnp.concatenate([tot[j * N_HEADS:(j + 1) * N_HEADS]] * reps, axis=0)
    car_s[...] = car
    _softmax_update(jnp.concatenate(s_list, axis=1), v_list, m_s, l_s, acc_s)

    @pl.when(step == pl.num_programs(1) - 1)
    def _():
        o_ref[0] = _kv_head_select(acc_s[...] / l_s[...])


def _fox_decode(pt_flat, qbd, kn, vn, lfn, cache_k, cache_v, cache_f, n_pages, n_pg, n_new):
    nb, rows, _ = qbd.shape
    assert n_pg * N_HEADS <= LANE
    per_b = lambda b, s, pt: (b, 0, 0)
    in_specs = [pl.BlockSpec((1, rows, KV_WIDTH), per_b), pl.BlockSpec((1, PAGE, KV_WIDTH), per_b),
                pl.BlockSpec((1, PAGE, KV_WIDTH), per_b), pl.BlockSpec((1, rows, LANE), per_b)]
    in_specs += [_page_spec(KV_WIDTH, j, n_pg, n_pages, True) for j in range(n_pg)] * 2
    in_specs += [_page_spec(N_HEADS, j, n_pg, n_pages, True) for j in range(n_pg)]
    grid_spec = pltpu.PrefetchScalarGridSpec(
        num_scalar_prefetch=1, grid=(nb, n_pages // n_pg), in_specs=in_specs,
        out_specs=pl.BlockSpec((1, rows, HEAD_DIM), per_b),
        scratch_shapes=[pltpu.VMEM((rows, 1), F32), pltpu.VMEM((rows, 1), F32), pltpu.VMEM((rows, KV_WIDTH), F32),
                        pltpu.VMEM((rows, 1), F32), pltpu.VMEM((PAGE, LANE), F32)])
    return pl.pallas_call(
        functools.partial(_fox_dec_kernel, n_pg=n_pg, n_new=n_new, scale=HEAD_DIM ** -0.5), grid_spec=grid_spec,
        out_shape=jax.ShapeDtypeStruct((nb, rows, HEAD_DIM), F32),
        compiler_params=_cparams(2), name="fox_decode")(
            pt_flat, qbd, kn, vn, lfn, *([cache_k] * n_pg), *([cache_v] * n_pg), *([cache_f] * n_pg))


def _rope_tables(pos):
    half = ROPE // 2
    inv = ROPE_THETA ** (-jnp.arange(half, dtype=F32) / half)
    ang = pos.astype(F32)[:, None] * inv[None, :]
    cos, sin = jnp.cos(ang), jnp.sin(ang)
    pad = LANE - ROPE
    return (jnp.concatenate([cos, cos, jnp.ones((pos.shape[0], pad), F32)], axis=1),
            jnp.concatenate([sin, sin, jnp.zeros((pos.shape[0], pad), F32)], axis=1))


def _rot_partner(w):
    half = w.shape[-1] // 2
    return jnp.concatenate([-w[..., half:], w[..., :half]], axis=-1)


def _zero_cols(w, n):
    return jnp.zeros(w.shape[:-1] + (n,), w.dtype)


def _block_diag_queries(q, nb, n_tok):
    q = q.reshape(nb, n_tok, N_HEADS, 1, HEAD_DIM)
    onehot = (jnp.arange(N_HEADS)[:, None] // GROUP == jnp.arange(KV_HEADS)[None, :]).astype(q.dtype)
    return (q * onehot[None, None, :, :, None]).reshape(nb, n_tok * N_HEADS, KV_WIDTH).astype(BF16)


def _pad_new_keys(x, nb, n_tok):
    x = x.astype(BF16).reshape(nb, n_tok, x.shape[-1])
    return jnp.pad(x, ((0, 0), (0, PAGE - n_tok), (0, 0)))


def _gate_layouts(g, nb, n_tok):
    g = g.reshape(nb, n_tok, g.shape[-1]).transpose(0, 2, 1)
    return g[:, :, None, :], g[:, :, :, None]


def kernel(x_prompt, x_sample, page_table, cache_a_ckv, cache_a_krope, cache_b_k, cache_b_v, cache_c_k, cache_c_v,
           cache_c_logf, state_d_C, state_d_n, state_d_m, ln_g, ln_b, a_w_in, a_g_cq, a_g_ckv, a_w_uq, a_w_uk,
           a_w_uv, a_w_out, b_w_in, b_w_out, c_w_in, c_b_f, c_w_out, d_w_in, d_b_i, d_b_f, d_w_out):
    batch, seq, _ = x_prompt.shape
    nb, n_tok, _ = x_sample.shape
    n_pages = page_table.shape[1]
    n_pool = cache_a_ckv.shape[0]
    n_pg = min(8, n_pages)
    mp, ms = batch * seq, nb * n_tok
    rows = n_tok * N_HEADS
    xp = x_prompt.reshape(mp, D_MODEL)
    xs = x_sample.reshape(ms, D_MODEL)
    xpb, xsb = xp.astype(BF16), xs.astype(BF16)
    pt_flat = page_table.reshape(-1)
    pos_p = jnp.tile(jnp.arange(seq, dtype=jnp.int32), batch)
    pos_s = n_pages * PAGE + jnp.tile(jnp.arange(n_tok, dtype=jnp.int32), nb)
    ln_g = ln_g.reshape(DEPTH, 1, D_MODEL)
    ln_b = ln_b.reshape(DEPTH, 1, D_MODEL)

    def proj(xb, w):
        tm, tn = _proj_tiles(xb.shape[0], w.shape[1])
        return _matmul(xb, w, tm, tn)

    w_cq, w_ckv = a_w_in[:, :LORA], a_w_in[:, LORA:2 * LORA]
    w_kr, w_gate = a_w_in[:, 2 * LORA:2 * LORA + ROPE], a_w_in[:, 2 * LORA + ROPE:]
    zpad = _zero_cols(w_kr, LANE - ROPE)
    w_in = jnp.concatenate([w_gate, w_cq, w_ckv, w_kr, zpad, _rot_partner(w_kr), zpad], axis=1).astype(BF16)
    uq_nope, uq_rope = a_w_uq[..., :HEAD_DIM], a_w_uq[..., HEAD_DIM:]
    zr = _zero_cols(uq_rope, LANE - ROPE)
    wq = jnp.concatenate([uq_nope.reshape(LORA, D_MODEL),
                          jnp.concatenate([uq_rope, zr], axis=-1).reshape(LORA, D_MODEL),
                          jnp.concatenate([_rot_partner(uq_rope), zr], axis=-1).reshape(LORA, D_MODEL)],
                         axis=1).astype(BF16)
    wkv = jnp.concatenate([a_w_uk.reshape(LORA, D_MODEL), a_w_uv.reshape(LORA, D_MODEL)], axis=1).astype(BF16)
    w_uk_t = a_w_uk.transpose(1, 2, 0).reshape(D_MODEL, LORA).astype(BF16)
    w_uv2 = a_w_uv.reshape(LORA, D_MODEL).astype(BF16)
    w_out = a_w_out.astype(BF16)
    g_cq, g_ckv = a_g_cq.reshape(1, LORA), a_g_ckv.reshape(1, LORA)
    cos_p, sin_p = _rope_tables(pos_p)
    cos_s, sin_s = _rope_tables(pos_s)

    pa = proj(xpb, w_in)
    q, ckvn_p, krr_p, kk, vv = _mla_mid(pa, g_cq, g_ckv, cos_p, sin_p, wq, wkv)
    o = _flash_prompt(q, kk, vv, batch=batch, seq=seq, group=1, dqk=QK_PAD, dv=HEAD_DIM, q_cb=0, k_cb=0, v_cb=0,
                      scale=(HEAD_DIM + ROPE) ** -0.5)
    xp, xpb = _out_ln(o, pa, xp, w_out, ln_g[0], ln_b[0])

    sa = proj(xsb, w_in)
    q_s, ckvn_s, krr_s = _mla_mid(sa, g_cq, g_ckv, cos_s, sin_s, wq, None)
    ql = _head_matmul(q_s, w_uk_t, x_width=HEAD_DIM, x_stride=2, w_rows=True, out_width=LORA, out_dtype=BF16)
    ql = ql.reshape(nb, rows, LORA)
    qr = q_s.reshape(ms, N_HEADS, QK_PAD)[:, :, HEAD_DIM:].reshape(nb, rows, LANE)
    o_lat = _mla_decode(pt_flat, ql, qr, _pad_new_keys(ckvn_s, nb, n_tok), _pad_new_keys(krr_s, nb, n_tok),
                        cache_a_ckv, cache_a_krope, n_pages, n_pg)
    o_s = _head_matmul(o_lat.reshape(ms, N_HEADS * LORA), w_uv2, x_width=LORA, x_stride=1, w_rows=False,
                       out_width=HEAD_DIM, out_dtype=F32)
    xs, xsb = _out_ln(o_s, sa, xs, w_out, ln_g[0], ln_b[0])
    a_ckv_p = ckvn_p.reshape(batch, seq, LORA)
    a_kr_p = krr_p[:, :ROPE].reshape(batch, seq, ROPE)
    a_ckv_s = ckvn_s.reshape(nb, n_tok, LORA)
    a_kr_s = krr_s[:, :ROPE].reshape(nb, n_tok, ROPE)

    kq, kk_, kv_ = D_MODEL, D_MODEL + KV_WIDTH, D_MODEL + 2 * KV_WIDTH
    w_in = jnp.concatenate([b_w_in[:, kv_:], b_w_in[:, :kv_]], axis=1).astype(BF16)
    w_out = b_w_out.astype(BF16)
    k_off, v_off = 2 * D_MODEL, 2 * D_MODEL + KV_WIDTH
    pb = proj(xpb, w_in)
    o = _sb_prompt(pb, batch=batch, seq=seq, q_cb=D_MODEL // KV_WIDTH, k_cb=k_off // HEAD_DIM, v_cb=v_off // HEAD_DIM)
    xp, xpb = _out_ln(o, pb, xp, w_out, ln_g[1], ln_b[1])
    sb = proj(xsb, w_in)
    k_new, v_new = sb[:, k_off:v_off], sb[:, v_off:v_off + KV_WIDTH]
    o_s = _sb_decode(pt_flat, _block_diag_queries(sb[:, D_MODEL:k_off], nb, n_tok),
                     _pad_new_keys(k_new, nb, n_tok), _pad_new_keys(v_new, nb, n_tok),
                     cache_b_k.reshape(n_pool, PAGE, KV_WIDTH), cache_b_v.reshape(n_pool, PAGE, KV_WIDTH),
                     n_pages, n_pg)
    xs, xsb = _out_ln(o_s.reshape(ms, D_MODEL), sb, xs, w_out, ln_g[1], ln_b[1])
    kv_shape_p = (batch, seq, KV_HEADS, HEAD_DIM)
    kv_shape_s = (nb, n_tok, KV_HEADS, HEAD_DIM)
    b_k_p = pb[:, k_off:v_off].reshape(kv_shape_p)
    b_v_p = pb[:, v_off:v_off + KV_WIDTH].reshape(kv_shape_p)
    b_k_s, b_v_s = k_new.reshape(kv_shape_s), v_new.reshape(kv_shape_s)

    f_off = v_off + KV_WIDTH
    w_f = c_w_in[:, kv_:kv_ + N_HEADS]
    w_in = jnp.concatenate([c_w_in[:, kv_ + N_HEADS:], c_w_in[:, :kv_], w_f, _zero_cols(w_f, 2 * LANE - N_HEADS)],
                           axis=1).astype(BF16)
    w_out = c_w_out.astype(BF16)
    bf_pad = jnp.pad(c_b_f, (0, LANE - N_HEADS)).reshape(1, LANE)
    pc = proj(xpb, w_in)
    lf_p, fc_p = _fox_gates(pc, bf_pad, batch=batch, seq=seq, f_cb=f_off // LANE)
    fk, fq = _gate_layouts(fc_p[:, :N_HEADS], batch, seq)
    o = _flash_prompt(pc, pc, pc, batch=batch, seq=seq, group=GROUP, dqk=HEAD_DIM, dv=HEAD_DIM,
                      q_cb=D_MODEL // HEAD_DIM, k_cb=k_off // HEAD_DIM, v_cb=v_off // HEAD_DIM,
                      scale=HEAD_DIM ** -0.5, fq=fq, fk=fk)
    xp, xpb = _out_ln(o, pc, xp, w_out, ln_g[2], ln_b[2])
    sc = proj(xsb, w_in)
    lf_s, _ = _fox_gates(sc, bf_pad, batch=1, seq=ms, f_cb=f_off // LANE)
    lf_new = lf_s[:, :N_HEADS].reshape(nb, n_tok, N_HEADS)
    lfn = jnp.broadcast_to(lf_new.transpose(0, 2, 1)[:, None], (nb, n_tok, N_HEADS, n_tok)).reshape(nb, rows, n_tok)
    lfn = jnp.pad(lfn, ((0, 0), (0, 0), (0, LANE - n_tok)))
    k_new, v_new = sc[:, k_off:v_off], sc[:, v_off:v_off + KV_WIDTH]
    o_s = _fox_decode(pt_flat, _block_diag_queries(sc[:, D_MODEL:k_off], nb, n_tok),
                      _pad_new_keys(k_new, nb, n_tok), _pad_new_keys(v_new, nb, n_tok), lfn,
                      cache_c_k.reshape(n_pool, PAGE, KV_WIDTH), cache_c_v.reshape(n_pool, PAGE, KV_WIDTH),
                      cache_c_logf, n_pages, n_pg, n_tok)
    xs, xsb = _out_ln(o_s.reshape(ms, D_MODEL), sc, xs, w_out, ln_g[2], ln_b[2])
    c_k_p = pc[:, k_off:v_off].reshape(kv_shape_p)
    c_v_p = pc[:, v_off:v_off + KV_WIDTH].reshape(kv_shape_p)
    c_logf_p = lf_p[:, :N_HEADS].reshape(batch, seq, N_HEADS)
    c_k_s, c_v_s = k_new.reshape(kv_shape_s), v_new.reshape(kv_shape_s)
    c_logf_s = lf_new

    hq, hv = ML_HEADS * ML_QK, ML_HEADS * ML_V
    w_q, w_k, w_v = d_w_in[:, :hq], d_w_in[:, hq:2 * hq], d_w_in[:, 2 * hq:2 * hq + hv]
    w_if = d_w_in[:, 2 * hq + hv:2 * hq + hv + 2 * ML_HEADS]
    w_op = d_w_in[:, 2 * hq + hv + 2 * ML_HEADS:2 * hq + 2 * hv + 2 * ML_HEADS]
    w_gate = d_w_in[:, 2 * hq + 2 * hv + 2 * ML_HEADS:]
    w_in = jnp.concatenate([w_gate, w_op, w_v, w_q, w_k, w_if, _zero_cols(w_if, 2 * LANE - 2 * ML_HEADS)],
                           axis=1).astype(BF16)
    w_out = d_w_out.astype(BF16)
    op_off, vd_off, qd_off, kd_off, if_off = D_MODEL, 2 * D_MODEL, 3 * D_MODEL, 3 * D_MODEL + hq, 3 * D_MODEL + 2 * hq
    cols = dict(q_cb=qd_off // ML_QK, k_cb=kd_off // ML_QK, v_cb=vd_off // ML_V, op_cb=op_off // ML_V)

    def ml_gates(p, n_seq, n_t):
        i_row, i_col = _gate_layouts(p[:, if_off:if_off + ML_HEADS], n_seq, n_t)
        f_row, f_col = _gate_layouts(p[:, if_off + ML_HEADS:if_off + 2 * ML_HEADS], n_seq, n_t)
        return i_row, f_row, i_col, f_col

    pd = proj(xpb, w_in)
    chunk = min(256, seq)
    n_chunks = seq // chunk
    h, d_C_p, d_n_p, d_m_p = _mlstm(
        pd.reshape(batch * n_chunks, chunk, pd.shape[1]), ml_gates(pd, batch, seq), d_b_i, d_b_f,
        jnp.zeros((batch, ML_HEADS, ML_QK, ML_V), F32), jnp.zeros((batch, ML_HEADS, 1, ML_QK), F32),
        jnp.full((batch, ML_HEADS, 1, 1), NEG_INF, F32), batch=batch, n_chunks=n_chunks, chunk=chunk, **cols)
    xp, xpb = _out_ln(h.reshape(mp, D_MODEL), pd, xp, w_out, ln_g[3], ln_b[3])
    sd = proj(xsb, w_in)
    h_s, d_C_s, d_n_s, d_m_s = _mlstm(
        sd.reshape(nb, n_tok, sd.shape[1]), ml_gates(sd, nb, n_tok), d_b_i, d_b_f,
        state_d_C, state_d_n.reshape(nb, ML_HEADS, 1, ML_QK), state_d_m.reshape(nb, ML_HEADS, 1, 1),
        batch=nb, n_chunks=1, chunk=PAGE, **cols)
    xs, xsb = _out_ln(h_s.reshape(ms, D_MODEL), sd, xs, w_out, ln_g[3], ln_b[3])

    return (xp.reshape(batch, seq, D_MODEL), xs.reshape(nb, n_tok, D_MODEL), a_ckv_p, a_kr_p, a_ckv_s, a_kr_s,
            b_k_p, b_v_p, b_k_s, b_v_s, c_k_p, c_v_p, c_logf_p, c_k_s, c_v_s, c_logf_s,
            d_C_p, d_n_p.reshape(batch, ML_HEADS, ML_QK), d_m_p.reshape(batch, ML_HEADS),
            d_C_s, d_n_s.reshape(nb, ML_HEADS, ML_QK), d_m_s.reshape(nb, ML_HEADS))
```

```python
import functools

import jax
import jax.numpy as jnp
import numpy as np
from jax import lax
from jax.experimental import pallas as pl
from jax.experimental.pallas import tpu as pltpu

F32 = jnp.float32
BF16 = jnp.bfloat16

D_MODEL = 2048
HEAD_DIM = 128
N_HEADS = D_MODEL // HEAD_DIM
KV_HEADS = N_HEADS // 4
GROUP = N_HEADS // KV_HEADS
KV_WIDTH = KV_HEADS * HEAD_DIM
PAGE = 128
DEPTH = 4
ALPHA = (2 * DEPTH) ** 0.25
LN_EPS = 1e-5
RMS_EPS = 1e-6
ROPE_THETA = 10000.0
LORA = D_MODEL // 4
ROPE = HEAD_DIM // 2
QK_PAD = 2 * HEAD_DIM
ML_HEADS = 4
ML_V = D_MODEL // ML_HEADS
ML_QK = ML_V // 2
LANE = 128
VMEM_LIMIT = 56 * 1024 * 1024
NEG_INF = float("-inf")


def _cparams(n_axes):
    return pltpu.CompilerParams(dimension_semantics=("arbitrary",) * n_axes,
                                vmem_limit_bytes=VMEM_LIMIT)


def _dot(a, b):
    return jnp.dot(a, b, preferred_element_type=F32)


def _dot_nt(a, b):
    return lax.dot_general(a, b, (((1,), (1,)), ((), ())), preferred_element_type=F32)


def _log_sigmoid(z):
    return jnp.minimum(z, 0.0) - jnp.log(1.0 + jnp.exp(-jnp.abs(z)))


def _sigmoid(z):
    return 1.0 / (1.0 + jnp.exp(-z))


def _split_bf16(x):
    hi = x.astype(BF16)
    lo = (x - hi.astype(F32)).astype(BF16)
    return hi, lo


def _tri(n, kind):
    r = lax.broadcasted_iota(jnp.int32, (n, n), 0)
    c = lax.broadcasted_iota(jnp.int32, (n, n), 1)
    if kind == "lower_incl":
        m = c <= r
    else:
        m = r > c
    return jnp.where(m, 1.0, 0.0).astype(BF16)


def _pair_tables(n, descending):
    qs, ks = [], []
    for qi in range(n):
        order = range(qi, -1, -1) if descending else range(qi + 1)
        for kj in order:
            qs.append(qi)
            ks.append(kj)
    return jnp.asarray(np.array(qs, np.int32)), jnp.asarray(np.array(ks, np.int32))


def _mm_kernel(x_ref, w_ref, o_ref):
    o_ref[...] = _dot(x_ref[...], w_ref[...])


def _matmul(xb, wb, tm, tn):
    m, k = xb.shape
    n = wb.shape[1]
    return pl.pallas_call(
        _mm_kernel, grid=(m // tm, n // tn),
        in_specs=[pl.BlockSpec((tm, k), lambda i, j: (i, 0)),
                  pl.BlockSpec((k, tn), lambda i, j: (0, j))],
        out_specs=pl.BlockSpec((tm, tn), lambda i, j: (i, j)),
        out_shape=jax.ShapeDtypeStruct((m, n), F32),
        compiler_params=_cparams(2), name="proj")(xb, wb)


def _proj_tiles(m, n):
    tm = max(c for c in (1024, 512, 256) if m % c == 0)
    units = n // 256
    d = max(c for c in (1, 2, 3, 4) if units % c == 0)
    return tm, 256 * d


def _out_ln_kernel(o_ref, gate_ref, x_ref, w_ref, g_ref, b_ref, xo_ref, xb_ref):
    g = gate_ref[...]
    a = (o_ref[...] * (g * _sigmoid(g))).astype(BF16)
    z = ALPHA * x_ref[...] + _dot(a, w_ref[...])
    mu = jnp.mean(z, axis=-1, keepdims=True)
    zc = z - mu
    var = jnp.mean(zc * zc, axis=-1, keepdims=True)
    out = zc * lax.rsqrt(var + LN_EPS) * g_ref[...] + b_ref[...]
    xo_ref[...] = out
    xb_ref[...] = out.astype(BF16)


def _out_ln(o, proj, x, wb, ln_g, ln_b):
    m = x.shape[0]
    tm = 256
    row = lambda i: (i, 0)
    fixed = lambda i: (0, 0)
    return pl.pallas_call(
        _out_ln_kernel, grid=(m // tm,),
        in_specs=[pl.BlockSpec((tm, D_MODEL), row), pl.BlockSpec((tm, D_MODEL), row),
                  pl.BlockSpec((tm, D_MODEL), row), pl.BlockSpec((D_MODEL, D_MODEL), fixed),
                  pl.BlockSpec((1, D_MODEL), fixed), pl.BlockSpec((1, D_MODEL), fixed)],
        out_specs=[pl.BlockSpec((tm, D_MODEL), row), pl.BlockSpec((tm, D_MODEL), row)],
        out_shape=[jax.ShapeDtypeStruct((m, D_MODEL), F32), jax.ShapeDtypeStruct((m, D_MODEL), BF16)],
        compiler_params=_cparams(1), name="out_ln")(o, proj, x, wb, ln_g, ln_b)


def _rms(x, g):
    return x * lax.rsqrt(jnp.mean(x * x, axis=-1, keepdims=True) + RMS_EPS) * g


def _mla_mid_kernel(cq_ref, ckv_ref, kr_ref, krp_ref, gq_ref, gkv_ref, cos_ref, sin_ref, wq_ref, *rest,
                    with_kv):
    if with_kv:
        wkv_ref, q_ref, ckvn_ref, krr_ref, k_ref, v_ref = rest
    else:
        q_ref, ckvn_ref, krr_ref = rest
    cos = cos_ref[...]
    sin = sin_ref[...]
    cqn = _rms(cq_ref[...], gq_ref[...]).astype(BF16)
    qa = _dot(cqn, wq_ref[...])
    krr = kr_ref[...] * cos + krp_ref[...] * sin
    krr_ref[...] = krr
    ckvn = _rms(ckv_ref[...], gkv_ref[...])
    ckvn_ref[...] = ckvn
    for h in range(N_HEADS):
        lo = h * HEAD_DIM
        q_ref[:, h * QK_PAD:h * QK_PAD + HEAD_DIM] = qa[:, lo:lo + HEAD_DIM].astype(BF16)
        qr = qa[:, D_MODEL + lo:D_MODEL + lo + HEAD_DIM] * cos + qa[:, 2 * D_MODEL + lo:2 * D_MODEL + lo + HEAD_DIM] * sin
        q_ref[:, h * QK_PAD + HEAD_DIM:(h + 1) * QK_PAD] = qr.astype(BF16)
    if with_kv:
        kv = _dot(ckvn.astype(BF16), wkv_ref[...])
        krb = krr.astype(BF16)
        for h in range(N_HEADS):
            lo = h * HEAD_DIM
            k_ref[:, h * QK_PAD:h * QK_PAD + HEAD_DIM] = kv[:, lo:lo + HEAD_DIM].astype(BF16)
            k_ref[:, h * QK_PAD + HEAD_DIM:(h + 1) * QK_PAD] = krb
        v_ref[...] = kv[:, D_MODEL:].astype(BF16)


def _mla_mid(proj, g_cq, g_ckv, cos_t, sin_t, wq, wkv):
    m = proj.shape[0]
    tm = 256
    with_kv = wkv is not None
    fixed = lambda i: (0, 0)
    in_specs = [pl.BlockSpec((tm, LORA), lambda i: (i, D_MODEL // LORA)),
                pl.BlockSpec((tm, LORA), lambda i: (i, D_MODEL // LORA + 1)),
                pl.BlockSpec((tm, LANE), lambda i: (i, (D_MODEL + 2 * LORA) // LANE)),
                pl.BlockSpec((tm, LANE), lambda i: (i, (D_MODEL + 2 * LORA) // LANE + 1)),
                pl.BlockSpec((1, LORA), fixed), pl.BlockSpec((1, LORA), fixed),
                pl.BlockSpec((tm, LANE), lambda i: (i, 0)), pl.BlockSpec((tm, LANE), lambda i: (i, 0)),
                pl.BlockSpec((LORA, 3 * D_MODEL), fixed)]
    args = [proj, proj, proj, proj, g_cq, g_ckv, cos_t, sin_t, wq]
    row = lambda i: (i, 0)
    out_specs = [pl.BlockSpec((tm, N_HEADS * QK_PAD), row), pl.BlockSpec((tm, LORA), row),
                 pl.BlockSpec((tm, LANE), row)]
    out_shape = [jax.ShapeDtypeStruct((m, N_HEADS * QK_PAD), BF16), jax.ShapeDtypeStruct((m, LORA), F32),
                 jax.ShapeDtypeStruct((m, LANE), F32)]
    if with_kv:
        in_specs.append(pl.BlockSpec((LORA, 2 * D_MODEL), fixed))
        args.append(wkv)
        out_specs += [pl.BlockSpec((tm, N_HEADS * QK_PAD), row), pl.BlockSpec((tm, D_MODEL), row)]
        out_shape += [jax.ShapeDtypeStruct((m, N_HEADS * QK_PAD), BF16), jax.ShapeDtypeStruct((m, D_MODEL), BF16)]
    return pl.pallas_call(
        functools.partial(_mla_mid_kernel, with_kv=with_kv), grid=(m // tm,),
        in_specs=in_specs, out_specs=out_specs, out_shape=out_shape,
        compiler_params=_cparams(1), name="mla_mid")(*args)


def _head_mm_kernel(x_ref, w_ref, o_ref):
    o_ref[...] = _dot(x_ref[...].astype(BF16), w_ref[...]).astype(o_ref.dtype)


def _head_matmul(x, w, *, x_width, x_stride, w_rows, out_width, out_dtype):
    m = x.shape[0]
    if w_rows:
        w_spec = pl.BlockSpec((x_width, out_width), lambda h: (h, 0))
    else:
        w_spec = pl.BlockSpec((x_width, out_width), lambda h: (0, h))
    return pl.pallas_call(
        _head_mm_kernel, grid=(N_HEADS,),
        in_specs=[pl.BlockSpec((m, x_width), lambda h: (0, h * x_stride)), w_spec],
        out_specs=pl.BlockSpec((m, out_width), lambda h: (0, h)),
        out_shape=jax.ShapeDtypeStruct((m, N_HEADS * out_width), out_dtype),
        compiler_params=_cparams(1), name="head_mm")(x, w)


def _flash_kernel(qt_ref, kt_ref, q_ref, k_ref, v_ref, *rest, scale, tq, has_bias):
    if has_bias:
        fq_ref, fk_ref, o_ref, m_s, l_s, acc_s = rest
    else:
        o_ref, m_s, l_s, acc_s = rest
    p = pl.program_id(2)
    qi = qt_ref[p]
    kj = kt_ref[p]

    @pl.when(kj == 0)
    def _():
        m_s[...] = jnp.full(m_s.shape, NEG_INF, F32)
        l_s[...] = jnp.zeros(l_s.shape, F32)
        acc_s[...] = jnp.zeros(acc_s.shape, F32)

    s = _dot_nt(q_ref[...].astype(BF16), k_ref[...].astype(BF16)) * scale
    if has_bias:
        s = s + (fq_ref[0, 0] - fk_ref[0, 0])

    def update(sm):
        m_old = m_s[...]
        m_new = jnp.maximum(m_old, jnp.max(sm, axis=1, keepdims=True))
        alpha = jnp.exp(m_old - m_new)
        pe = jnp.exp(sm - m_new)
        l_s[...] = alpha * l_s[...] + jnp.sum(pe, axis=1, keepdims=True)
        acc_s[...] = alpha * acc_s[...] + _dot(pe.astype(BF16), v_ref[...].astype(BF16))
        m_s[...] = m_new

    @pl.when(kj < qi)
    def _():
        update(s)

    @pl.when(kj == qi)
    def _():
        row = lax.broadcasted_iota(jnp.int32, s.shape, 0)
        col = lax.broadcasted_iota(jnp.int32, s.shape, 1)
        update(jnp.where(col <= row, s, NEG_INF))
        o_ref[...] = acc_s[...] / l_s[...]


def _flash_prompt(q, k, v, *, batch, seq, group, dqk, dv, q_cb, k_cb, v_cb, scale, fq=None, fk=None, tq=1024):
    tq = min(tq, seq)
    nq = seq // tq
    qt, kt = _pair_tables(nq, descending=False)
    has_bias = fq is not None
    in_specs = [pl.BlockSpec((tq, dqk), lambda b, h, p, qt, kt: (b * nq + qt[p], q_cb + h)),
                pl.BlockSpec((tq, dqk), lambda b, h, p, qt, kt: (b * nq + kt[p], k_cb + h // group)),
                pl.BlockSpec((tq, dv), lambda b, h, p, qt, kt: (b * nq + kt[p], v_cb + h // group))]
    args = [q, k, v]
    if has_bias:
        in_specs += [pl.BlockSpec((1, 1, tq, 1), lambda b, h, p, qt, kt: (b, h, qt[p], 0)),
                     pl.BlockSpec((1, 1, 1, tq), lambda b, h, p, qt, kt: (b, h, 0, kt[p]))]
        args += [fq, fk]
    grid_spec = pltpu.PrefetchScalarGridSpec(
        num_scalar_prefetch=2, grid=(batch, N_HEADS, int(qt.shape[0])), in_specs=in_specs,
        out_specs=pl.BlockSpec((tq, dv), lambda b, h, p, qt, kt: (b * nq + qt[p], h)),
        scratch_shapes=[pltpu.VMEM((tq, 1), F32), pltpu.VMEM((tq, 1), F32), pltpu.VMEM((tq, dv), F32)])
    return pl.pallas_call(
        functools.partial(_flash_kernel, scale=scale, tq=tq, has_bias=has_bias), grid_spec=grid_spec,
        out_shape=jax.ShapeDtypeStruct((batch * seq, N_HEADS * dv), F32),
        compiler_params=_cparams(3), name="flash_prompt")(qt, kt, *args)


def _sb_kernel(qt_ref, kt_ref, q_ref, k_ref, v_ref, later_ref, o_ref, qs_s, acc_s, car_s, *, tq, scale):
    p = pl.program_id(2)
    qi = qt_ref[p]
    kj = kt_ref[p]

    def step(diagonal):
        z = _dot_nt(qs_s[...], k_ref[...].astype(BF16)) * scale
        ls = _log_sigmoid(z)
        lk = ls - z
        if diagonal:
            strict = lax.broadcasted_iota(jnp.int32, z.shape, 1) < (lax.broadcasted_iota(jnp.int32, z.shape, 0) & (tq - 1))
            lk = jnp.where(strict, lk, 0.0)
        hi, lo = _split_bf16(lk)
        later = later_ref[...]
        sfx = _dot(hi, later) + _dot(lo, later)
        w = jnp.exp(ls + sfx + car_s[...])
        if diagonal:
            w = jnp.where(strict, w, 0.0)
        acc_s[...] += _dot(w.astype(BF16), v_ref[...].astype(BF16))
        car_s[...] += (sfx + lk)[:, 0:1]

    @pl.when(kj == qi)
    def _():
        for g in range(GROUP):
            qs_s[g * tq:(g + 1) * tq, :] = q_ref[:, g * HEAD_DIM:(g + 1) * HEAD_DIM].astype(BF16)
        acc_s[...] = jnp.zeros(acc_s.shape, F32)
        car_s[...] = jnp.zeros(car_s.shape, F32)
        step(True)

    @pl.when(kj < qi)
    def _():
        step(False)

    @pl.when(kj == 0)
    def _():
        for g in range(GROUP):
            o_ref[:, g * HEAD_DIM:(g + 1) * HEAD_DIM] = acc_s[g * tq:(g + 1) * tq, :]


def _sb_prompt(proj, *, batch, seq, q_cb, k_cb, v_cb, tq=256):
    tq = min(tq, seq)
    nq = seq // tq
    qt, kt = _pair_tables(nq, descending=True)
    grid_spec = pltpu.PrefetchScalarGridSpec(
        num_scalar_prefetch=2, grid=(batch, KV_HEADS, int(qt.shape[0])),
        in_specs=[pl.BlockSpec((tq, KV_WIDTH), lambda b, h, p, qt, kt: (b * nq + qt[p], q_cb + h)),
                  pl.BlockSpec((tq, HEAD_DIM), lambda b, h, p, qt, kt: (b * nq + kt[p], k_cb + h)),
                  pl.BlockSpec((tq, HEAD_DIM), lambda b, h, p, qt, kt: (b * nq + kt[p], v_cb + h)),
                  pl.BlockSpec((tq, tq), lambda b, h, p, qt, kt: (0, 0))],
        out_specs=pl.BlockSpec((tq, KV_WIDTH), lambda b, h, p, qt, kt: (b * nq + qt[p], h)),
        scratch_shapes=[pltpu.VMEM((GROUP * tq, HEAD_DIM), BF16), pltpu.VMEM((GROUP * tq, HEAD_DIM), F32),
                        pltpu.VMEM((GROUP * tq, 1), F32)])
    return pl.pallas_call(
        functools.partial(_sb_kernel, tq=tq, scale=HEAD_DIM ** -0.5), grid_spec=grid_spec,
        out_shape=jax.ShapeDtypeStruct((batch * seq, D_MODEL), F32),
        compiler_params=_cparams(3), name="sb_prompt")(qt, kt, proj, proj, proj, _tri(tq, "later"))


def _fox_gates_kernel(f_ref, bf_ref, lf_ref, fc_ref, carry_s, *, tb):
    @pl.when(pl.program_id(1) == 0)
    def _():
        carry_s[...] = jnp.zeros(carry_s.shape, F32)

    lf = _log_sigmoid(f_ref[...] + bf_ref[...])
    hi, lo = _split_bf16(lf)
    tri = _tri(tb, "lower_incl")
    cum = _dot(tri, hi) + _dot(tri, lo) + carry_s[...]
    lf_ref[...] = lf
    fc_ref[...] = cum
    carry_s[...] = cum[tb - 1:tb, :]


def _fox_gates(proj, bf_pad, *, batch, seq, f_cb, tb=512):
    tb = min(tb, seq)
    nb = seq // tb
    spec = pl.BlockSpec((tb, LANE), lambda b, t: (b * nb + t, 0))
    return pl.pallas_call(
        functools.partial(_fox_gates_kernel, tb=tb), grid=(batch, nb),
        in_specs=[pl.BlockSpec((tb, LANE), lambda b, t: (b * nb + t, f_cb)), pl.BlockSpec((1, LANE), lambda b, t: (0, 0))],
        out_specs=[spec, spec],
        out_shape=[jax.ShapeDtypeStruct((batch * seq, LANE), F32)] * 2,
        scratch_shapes=[pltpu.VMEM((1, LANE), F32)],
        compiler_params=_cparams(2), name="fox_gates")(proj, bf_pad)


def _mlstm_kernel(bi_ref, bf_ref, q_ref, k_ref, v_ref, op_ref, ir_ref, fr_ref, ic_ref, fc_ref,
                  c0_ref, n0_ref, m0_ref, h_ref, c1_ref, n1_ref, m1_ref, c_s, n_s, m_s, *pad_s,
                  chunk, rows_in, n_chunks):
    head = pl.program_id(1)
    ci = pl.program_id(2)

    @pl.when(ci == 0)
    def _():
        c_s[...] = c0_ref[0, 0]
        n_s[...] = n0_ref[0, 0]
        m_s[...] = m0_ref[0, 0]

    b_i = bi_ref[head]
    b_f = bf_ref[head]
    if rows_in == chunk:
        q, k, v = q_ref[0], k_ref[0], v_ref[0]
        i_row, f_row = ir_ref[0, 0], fr_ref[0, 0]
        i_col, f_col = ic_ref[0, 0], fc_ref[0, 0]
    else:
        q_p, k_p, v_p, ir_p, fr_p, ic_p, fc_p = pad_s
        for buf, src in ((q_p, q_ref[0]), (k_p, k_ref[0]), (v_p, v_ref[0])):
            buf[...] = jnp.zeros(buf.shape, F32)
            buf[0:rows_in, :] = src
        for buf, src in ((ir_p, ir_ref[0, 0]), (fr_p, fr_ref[0, 0])):
            buf[...] = jnp.zeros(buf.shape, F32)
            buf[:, 0:rows_in] = src
        for buf, src in ((ic_p, ic_ref[0, 0]), (fc_p, fc_ref[0, 0])):
            buf[...] = jnp.zeros(buf.shape, F32)
            buf[0:rows_in, :] = src
        q, k, v = q_p[...], k_p[...], v_p[...]
        i_row, f_row, i_col, f_col = ir_p[...], fr_p[...], ic_p[...], fc_p[...]
    ig_row = i_row + b_i
    lf_row = _log_sigmoid(f_row + b_f)
    ig_col = i_col + b_i
    lf_col = _log_sigmoid(f_col + b_f)
    if rows_in != chunk:
        live_row = lax.broadcasted_iota(jnp.int32, (1, chunk), 1) < rows_in
        live_col = lax.broadcasted_iota(jnp.int32, (chunk, 1), 0) < rows_in
        ig_row = jnp.where(live_row, ig_row, NEG_INF)
        lf_row = jnp.where(live_row, lf_row, 0.0)
        ig_col = jnp.where(live_col, ig_col, NEG_INF)
        lf_col = jnp.where(live_col, lf_col, 0.0)
    t_i = lax.broadcasted_iota(jnp.int32, (chunk, chunk), 0)
    s_i = lax.broadcasted_iota(jnp.int32, (chunk, chunk), 1)
    causal = s_i <= t_i
    bcum_col = jnp.sum(jnp.where(causal, lf_row, 0.0), axis=1, keepdims=True)
    bcum_row = jnp.sum(jnp.where(t_i <= s_i, lf_col, 0.0), axis=0, keepdims=True)
    d = jnp.where(causal, bcum_col - bcum_row + ig_row, NEG_INF)
    m0 = m_s[...]
    inter = bcum_col + m0
    m = jnp.maximum(inter, jnp.max(d, axis=1, keepdims=True))
    w_intra = jnp.exp(d - m)
    w_inter = jnp.exp(inter - m)
    qs = q * (ML_QK ** -0.5)
    qb = qs.astype(BF16)
    vb = v.astype(BF16)
    a = w_intra * _dot_nt(qb, k.astype(BF16))
    c_old = c_s[...]
    n_old = n_s[...]
    num = _dot(a.astype(BF16), vb) + w_inter * _dot(qb, c_old.astype(BF16))
    den = jnp.sum(a, axis=1, keepdims=True) + w_inter * jnp.sum(qs * n_old, axis=1, keepdims=True)
    hval = num / jnp.maximum(jnp.abs(den), jnp.exp(-m))
    h_ref[0] = hval[0:rows_in] * _sigmoid(op_ref[0])
    m_end = m[chunk - 1:chunk, :]
    b_last = bcum_col[chunk - 1:chunk, :]
    w_end = jnp.exp(b_last - bcum_col + ig_col - m_end)
    w_carry = jnp.exp(b_last + m0 - m_end)
    kw = k * w_end
    c_new = w_carry * c_old + _dot(jnp.transpose(kw).astype(BF16), vb)
    n_new = w_carry * n_old + jnp.sum(kw, axis=0, keepdims=True)
    c_s[...] = c_new
    n_s[...] = n_new
    m_s[...] = m_end

    @pl.when(ci == n_chunks - 1)
    def _():
        c1_ref[0, 0] = c_new
        n1_ref[0, 0] = n_new
        m1_ref[0, 0] = m_end


def _mlstm(proj3, gates, b_i, b_f, c0, n0, m0, *, batch, n_chunks, chunk, q_cb, k_cb, v_cb, op_cb):
    rows_in = proj3.shape[1]
    i_row, f_row, i_col, f_col = gates
    tok = lambda col: (lambda b, h, c, bi, bf: (b * n_chunks + c, 0, col + h))
    st = lambda b, h, c, bi, bf: (b, h, 0, 0)
    in_specs = [pl.BlockSpec((1, rows_in, ML_QK), tok(q_cb)), pl.BlockSpec((1, rows_in, ML_QK), tok(k_cb)),
                pl.BlockSpec((1, rows_in, ML_V), tok(v_cb)), pl.BlockSpec((1, rows_in, ML_V), tok(op_cb)),
                pl.BlockSpec((1, 1, 1, rows_in), lambda b, h, c, bi, bf: (b, h, 0, c)),
                pl.BlockSpec((1, 1, 1, rows_in), lambda b, h, c, bi, bf: (b, h, 0, c)),
                pl.BlockSpec((1, 1, rows_in, 1), lambda b, h, c, bi, bf: (b, h, c, 0)),
                pl.BlockSpec((1, 1, rows_in, 1), lambda b, h, c, bi, bf: (b, h, c, 0)),
                pl.BlockSpec((1, 1, ML_QK, ML_V), st), pl.BlockSpec((1, 1, 1, ML_QK), st),
                pl.BlockSpec((1, 1, 1, 1), st)]
    out_specs = [pl.BlockSpec((1, rows_in, ML_V), tok(0)),
                 pl.BlockSpec((1, 1, ML_QK, ML_V), st), pl.BlockSpec((1, 1, 1, ML_QK), st),
                 pl.BlockSpec((1, 1, 1, 1), st)]
    out_shape = [jax.ShapeDtypeStruct((proj3.shape[0], rows_in, D_MODEL), F32),
                 jax.ShapeDtypeStruct((batch, ML_HEADS, ML_QK, ML_V), F32),
                 jax.ShapeDtypeStruct((batch, ML_HEADS, 1, ML_QK), F32),
                 jax.ShapeDtypeStruct((batch, ML_HEADS, 1, 1), F32)]
    scratch = [pltpu.VMEM((ML_QK, ML_V), F32), pltpu.VMEM((1, ML_QK), F32), pltpu.VMEM((1, 1), F32)]
    if rows_in != chunk:
        scratch += [pltpu.VMEM((chunk, ML_QK), F32), pltpu.VMEM((chunk, ML_QK), F32), pltpu.VMEM((chunk, ML_V), F32),
                    pltpu.VMEM((1, chunk), F32), pltpu.VMEM((1, chunk), F32),
                    pltpu.VMEM((chunk, 1), F32), pltpu.VMEM((chunk, 1), F32)]
    grid_spec = pltpu.PrefetchScalarGridSpec(
        num_scalar_prefetch=2, grid=(batch, ML_HEADS, n_chunks), in_specs=in_specs, out_specs=out_specs,
        scratch_shapes=scratch)
    return pl.pallas_call(
        functools.partial(_mlstm_kernel, chunk=chunk, rows_in=rows_in, n_chunks=n_chunks), grid_spec=grid_spec,
        out_shape=out_shape, compiler_params=_cparams(3), name="mlstm")(
            b_i, b_f, proj3, proj3, proj3, proj3, i_row, f_row, i_col, f_col, c0, n0, m0)


def _page_spec(page_shape, j, n_pg, n_pages, reverse):
    def imap(b, s, pt):
        pg = n_pages - 1 - (s * n_pg + j) if reverse else s * n_pg + j
        return (pt[b * n_pages + pg], 0, 0)
    return pl.BlockSpec((1,) + page_shape, imap)


def _kv_page(ref):
    return jnp.concatenate([ref[0, pl.ds(kv, PAGE, stride=KV_HEADS), :] for kv in range(KV_HEADS)], axis=1)


def _softmax_update(s, vb, m_s, l_s, acc_s):
    m_old = m_s[...]
    m_new = jnp.maximum(m_old, jnp.max(s, axis=1, keepdims=True))
    alpha = jnp.exp(m_old - m_new)
    pe = jnp.exp(s - m_new)
    l_s[...] = alpha * l_s[...] + jnp.sum(pe, axis=1, keepdims=True)
    acc_s[...] = alpha * acc_s[...] + _dot(pe.astype(BF16), vb)
    m_s[...] = m_new


def _kv_pages(refs):
    return jnp.concatenate([_kv_page(r).astype(BF16) for r in refs], axis=0)


def _softmax_init(m_s, l_s, acc_s):
    m_s[...] = jnp.full(m_s.shape, NEG_INF, F32)
    l_s[...] = jnp.zeros(l_s.shape, F32)
    acc_s[...] = jnp.zeros(acc_s.shape, F32)


def _kv_head_select(acc):
    rows = acc.shape[0]
    kv_of_row = (lax.broadcasted_iota(jnp.int32, (rows, HEAD_DIM), 0) % N_HEADS) // GROUP
    out = jnp.zeros((rows, HEAD_DIM), F32)
    for kv in range(KV_HEADS):
        out = jnp.where(kv_of_row == kv, acc[:, kv * HEAD_DIM:(kv + 1) * HEAD_DIM], out)
    return out


def _mla_dec_kernel(pt_ref, ql_ref, qr_ref, cn_ref, kn_ref, *rest, n_pg, scale):
    c_refs, r_refs = rest[:n_pg], rest[n_pg:2 * n_pg]
    o_ref, m_s, l_s, acc_s, kr_s = rest[2 * n_pg:]
    step = pl.program_id(1)
    ql = ql_ref[0]
    qr = qr_ref[0]

    @pl.when(step == 0)
    def _():
        _softmax_init(m_s, l_s, acc_s)
        kr_s[...] = jnp.zeros(kr_s.shape, F32)
        cn = cn_ref[0]
        s = (_dot_nt(ql, cn) + _dot_nt(qr, kn_ref[0])) * scale
        tok = lax.broadcasted_iota(jnp.int32, s.shape, 0) // N_HEADS
        key = lax.broadcasted_iota(jnp.int32, s.shape, 1)
        _softmax_update(jnp.where(key <= tok, s, NEG_INF), cn, m_s, l_s, acc_s)

    for j in range(n_pg):
        kr_s[0:ROPE, j * PAGE:(j + 1) * PAGE] = r_refs[j][0]
    cb = jnp.concatenate([c_refs[j][0].astype(BF16) for j in range(n_pg)], axis=0)
    s = (_dot_nt(ql, cb) + _dot(qr, kr_s[...].astype(BF16))) * scale
    _softmax_update(s, cb, m_s, l_s, acc_s)

    @pl.when(step == pl.num_programs(1) - 1)
    def _():
        o_ref[0] = acc_s[...] / l_s[...]


def _mla_decode(pt_flat, ql, qr, cn, kn, cache_ckv, cache_kr, n_pages, n_pg):
    nb, rows, _ = ql.shape
    per_b = lambda b, s, pt: (b, 0, 0)
    in_specs = [pl.BlockSpec((1, rows, LORA), per_b), pl.BlockSpec((1, rows, LANE), per_b),
                pl.BlockSpec((1, PAGE, LORA), per_b), pl.BlockSpec((1, PAGE, LANE), per_b)]
    in_specs += [_page_spec((PAGE, LORA), j, n_pg, n_pages, False) for j in range(n_pg)]
    in_specs += [_page_spec((ROPE, PAGE), j, n_pg, n_pages, False) for j in range(n_pg)]
    grid_spec = pltpu.PrefetchScalarGridSpec(
        num_scalar_prefetch=1, grid=(nb, n_pages // n_pg), in_specs=in_specs,
        out_specs=pl.BlockSpec((1, rows, LORA), per_b),
        scratch_shapes=[pltpu.VMEM((rows, 1), F32), pltpu.VMEM((rows, 1), F32), pltpu.VMEM((rows, LORA), F32),
                        pltpu.VMEM((LANE, n_pg * PAGE), F32)])
    return pl.pallas_call(
        functools.partial(_mla_dec_kernel, n_pg=n_pg, scale=(HEAD_DIM + ROPE) ** -0.5), grid_spec=grid_spec,
        out_shape=jax.ShapeDtypeStruct((nb, rows, LORA), F32),
        compiler_params=_cparams(2), name="mla_decode")(
            pt_flat, ql, qr, cn, kn, *([cache_ckv] * n_pg), *([cache_kr] * n_pg))


def _sb_dec_kernel(pt_ref, q_ref, kn_ref, vn_ref, *rest, n_pg, scale):
    k_refs, v_refs = rest[:n_pg], rest[n_pg:2 * n_pg]
    o_ref, acc_s, car_s = rest[2 * n_pg:]
    step = pl.program_id(1)
    qb = q_ref[0]
    rows = qb.shape[0]
    later = _tri(PAGE, "later")

    @pl.when(step == 0)
    def _():
        z = _dot_nt(qb, kn_ref[0]) * scale
        tok = lax.broadcasted_iota(jnp.int32, z.shape, 0) // N_HEADS
        key = lax.broadcasted_iota(jnp.int32, z.shape, 1)
        strict = key < tok
        ls = _log_sigmoid(z)
        lk = jnp.where(strict, ls - z, 0.0)
        hi, lo = _split_bf16(lk)
        sfx = _dot(hi, later) + _dot(lo, later)
        w = jnp.where(strict, jnp.exp(ls + sfx), 0.0).astype(BF16)
        acc_s[...] = _dot(w, vn_ref[0])
        car_s[...] = (sfx + lk)[:, 0:1]

    z = _dot_nt(qb, _kv_pages(k_refs)) * scale
    ls = _log_sigmoid(z)
    lk = ls - z
    page = lambda x, j: x[:, j * PAGE:(j + 1) * PAGE]
    hi, lo = _split_bf16(jnp.concatenate([page(lk, j) for j in range(n_pg)], axis=0))
    sfx_all = _dot(hi, later) + _dot(lo, later)
    car = car_s[...]
    log_w = []
    for j in range(n_pg):
        sfx = sfx_all[j * rows:(j + 1) * rows]
        log_w.append(page(ls, j) + sfx + car)
        car = car + (sfx + page(lk, j))[:, 0:1]
    w = jnp.exp(jnp.concatenate(log_w, axis=1)).astype(BF16)
    acc = acc_s[...] + _dot(w, _kv_pages(v_refs))
    acc_s[...] = acc
    car_s[...] = car

    @pl.when(step == pl.num_programs(1) - 1)
    def _():
        o_ref[0] = _kv_head_select(acc)


def _sb_decode(pt_flat, qbd, kn, vn, cache_k, cache_v, n_pages, n_pg):
    nb, rows, _ = qbd.shape
    per_b = lambda b, s, pt: (b, 0, 0)
    in_specs = [pl.BlockSpec((1, rows, KV_WIDTH), per_b), pl.BlockSpec((1, PAGE, KV_WIDTH), per_b),
                pl.BlockSpec((1, PAGE, KV_WIDTH), per_b)]
    in_specs += [_page_spec((PAGE * KV_HEADS, HEAD_DIM), j, n_pg, n_pages, True) for j in range(n_pg)] * 2
    grid_spec = pltpu.PrefetchScalarGridSpec(
        num_scalar_prefetch=1, grid=(nb, n_pages // n_pg), in_specs=in_specs,
        out_specs=pl.BlockSpec((1, rows, HEAD_DIM), per_b),
        scratch_shapes=[pltpu.VMEM((rows, KV_WIDTH), F32), pltpu.VMEM((rows, 1), F32)])
    return pl.pallas_call(
        functools.partial(_sb_dec_kernel, n_pg=n_pg, scale=HEAD_DIM ** -0.5), grid_spec=grid_spec,
        out_shape=jax.ShapeDtypeStruct((nb, rows, HEAD_DIM), F32),
        compiler_params=_cparams(2), name="sb_decode")(
            pt_flat, qbd, kn, vn, *([cache_k] * n_pg), *([cache_v] * n_pg))


def _fox_dec_kernel(pt_ref, q_ref, kn_ref, vn_ref, lfn_ref, *rest, n_pg, n_new, scale):
    k_refs, v_refs, f_refs = rest[:n_pg], rest[n_pg:2 * n_pg], rest[2 * n_pg:3 * n_pg]
    o_ref, m_s, l_s, acc_s, car_s, lf_s = rest[3 * n_pg:]
    step = pl.program_id(1)
    qb = q_ref[0]
    rows = qb.shape[0]
    reps = rows // N_HEADS

    @pl.when(step == 0)
    def _():
        _softmax_init(m_s, l_s, acc_s)
        lf_s[...] = jnp.zeros(lf_s.shape, F32)
        lfn = lfn_ref[0]
        tok = lax.broadcasted_iota(jnp.int32, lfn.shape, 0) // N_HEADS
        key = lax.broadcasted_iota(jnp.int32, lfn.shape, 1)
        car_s[...] = jnp.sum(jnp.where(key <= tok, lfn, 0.0), axis=1, keepdims=True)
        dec = jnp.zeros(lfn.shape, F32)
        for jp in range(n_new):
            col = jnp.sum(jnp.where((key > jp) & (key <= tok), lfn, 0.0), axis=1, keepdims=True)
            dec = jnp.where(key == jp, col, dec)
        s = _dot_nt(qb, kn_ref[0]) * scale + dec
        _softmax_update(jnp.where(key <= tok, s, NEG_INF), vn_ref[0], m_s, l_s, acc_s)

    for j in range(n_pg):
        lf_s[j * N_HEADS:(j + 1) * N_HEADS, :] = f_refs[j][0]
    lft = lf_s[...]
    hi, lo = _split_bf16(lft)
    later = _tri(PAGE, "later")
    sfx = _dot(hi, later) + _dot(lo, later)
    tot = (sfx + lft)[:, 0:1]
    car = car_s[...]
    decay = []
    for j in range(n_pg):
        sfx_j = jnp.concatenate([sfx[j * N_HEADS:(j + 1) * N_HEADS]] * reps, axis=0)
        decay.append(sfx_j + car)
        car = car + jnp.concatenate([tot[j * N_HEADS:(j + 1) * N_HEADS]] * reps, axis=0)
    car_s[...] = car
    s = _dot_nt(qb, _kv_pages(k_refs)) * scale + jnp.concatenate(decay, axis=1)
    _softmax_update(s, _kv_pages(v_refs), m_s, l_s, acc_s)

    @pl.when(step == pl.num_programs(1) - 1)
    def _():
        o_ref[0] = _kv_head_select(acc_s[...] / l_s[...])


def _fox_decode(pt_flat, qbd, kn, vn, lfn, cache_k, cache_v, cache_f, n_pages, n_pg, n_new):
    nb, rows, _ = qbd.shape
    per_b = lambda b, s, pt: (b, 0, 0)
    in_specs = [pl.BlockSpec((1, rows, KV_WIDTH), per_b), pl.BlockSpec((1, PAGE, KV_WIDTH), per_b),
                pl.BlockSpec((1, PAGE, KV_WIDTH), per_b), pl.BlockSpec((1, rows, LANE), per_b)]
    in_specs += [_page_spec((PAGE * KV_HEADS, HEAD_DIM), j, n_pg, n_pages, True) for j in range(n_pg)] * 2
    in_specs += [_page_spec((N_HEADS, PAGE), j, n_pg, n_pages, True) for j in range(n_pg)]
    grid_spec = pltpu.PrefetchScalarGridSpec(
        num_scalar_prefetch=1, grid=(nb, n_pages // n_pg), in_specs=in_specs,
        out_specs=pl.BlockSpec((1, rows, HEAD_DIM), per_b),
        scratch_shapes=[pltpu.VMEM((rows, 1), F32), pltpu.VMEM((rows, 1), F32), pltpu.VMEM((rows, KV_WIDTH), F32),
                        pltpu.VMEM((rows, 1), F32), pltpu.VMEM((n_pg * N_HEADS, PAGE), F32)])
    return pl.pallas_call(
        functools.partial(_fox_dec_kernel, n_pg=n_pg, n_new=n_new, scale=HEAD_DIM ** -0.5), grid_spec=grid_spec,
        out_shape=jax.ShapeDtypeStruct((nb, rows, HEAD_DIM), F32),
        compiler_params=_cparams(2), name="fox_decode")(
            pt_flat, qbd, kn, vn, lfn, *([cache_k] * n_pg), *([cache_v] * n_pg), *([cache_f] * n_pg))


def _rope_tables(pos):
    half = ROPE // 2
    inv = ROPE_THETA ** (-jnp.arange(half, dtype=F32) / half)
    ang = pos.astype(F32)[:, None] * inv[None, :]
    cos, sin = jnp.cos(ang), jnp.sin(ang)
    pad = LANE - ROPE
    return (jnp.concatenate([cos, cos, jnp.ones((pos.shape[0], pad), F32)], axis=1),
            jnp.concatenate([sin, sin, jnp.zeros((pos.shape[0], pad), F32)], axis=1))


def _rot_partner(w):
    half = w.shape[-1] // 2
    return jnp.concatenate([-w[..., half:], w[..., :half]], axis=-1)


def _zero_cols(w, n):
    return jnp.zeros(w.shape[:-1] + (n,), w.dtype)


def _block_diag_queries(q, nb, n_tok):
    q = q.reshape(nb, n_tok, N_HEADS, 1, HEAD_DIM)
    onehot = (jnp.arange(N_HEADS)[:, None] // GROUP == jnp.arange(KV_HEADS)[None, :]).astype(q.dtype)
    return (q * onehot[None, None, :, :, None]).reshape(nb, n_tok * N_HEADS, KV_WIDTH).astype(BF16)


def _pad_new_keys(x, nb, n_tok):
    x = x.astype(BF16).reshape(nb, n_tok, x.shape[-1])
    return jnp.pad(x, ((0, 0), (0, PAGE - n_tok), (0, 0)))


def _gate_layouts(g, nb, n_tok):
    g = g.reshape(nb, n_tok, g.shape[-1]).transpose(0, 2, 1)
    return g[:, :, None, :], g[:, :, :, None]


def kernel(x_prompt, x_sample, page_table, cache_a_ckv, cache_a_krope, cache_b_k, cache_b_v, cache_c_k, cache_c_v,
           cache_c_logf, state_d_C, state_d_n, state_d_m, ln_g, ln_b, a_w_in, a_g_cq, a_g_ckv, a_w_uq, a_w_uk,
           a_w_uv, a_w_out, b_w_in, b_w_out, c_w_in, c_b_f, c_w_out, d_w_in, d_b_i, d_b_f, d_w_out):
    batch, seq, _ = x_prompt.shape
    nb, n_tok, _ = x_sample.shape
    n_pages = page_table.shape[1]
    n_pool = cache_a_ckv.shape[0]
    n_pg = min(16, n_pages)
    kv_rows = lambda c: c.reshape(n_pool, PAGE * KV_HEADS, HEAD_DIM)
    keys_last = lambda c: jnp.swapaxes(c, 1, 2)
    mp, ms = batch * seq, nb * n_tok
    rows = n_tok * N_HEADS
    xp = x_prompt.reshape(mp, D_MODEL)
    xs = x_sample.reshape(ms, D_MODEL)
    xpb, xsb = xp.astype(BF16), xs.astype(BF16)
    pt_flat = page_table.reshape(-1)
    pos_p = jnp.tile(jnp.arange(seq, dtype=jnp.int32), batch)
    pos_s = n_pages * PAGE + jnp.tile(jnp.arange(n_tok, dtype=jnp.int32), nb)
    ln_g = ln_g.reshape(DEPTH, 1, D_MODEL)
    ln_b = ln_b.reshape(DEPTH, 1, D_MODEL)

    def proj(xb, w):
        tm, tn = _proj_tiles(xb.shape[0], w.shape[1])
        return _matmul(xb, w, tm, tn)

    w_cq, w_ckv = a_w_in[:, :LORA], a_w_in[:, LORA:2 * LORA]
    w_kr, w_gate = a_w_in[:, 2 * LORA:2 * LORA + ROPE], a_w_in[:, 2 * LORA + ROPE:]
    zpad = _zero_cols(w_kr, LANE - ROPE)
    w_in = jnp.concatenate([w_gate, w_cq, w_ckv, w_kr, zpad, _rot_partner(w_kr), zpad], axis=1).astype(BF16)
    uq_nope, uq_rope = a_w_uq[..., :HEAD_DIM], a_w_uq[..., HEAD_DIM:]
    zr = _zero_cols(uq_rope, LANE - ROPE)
    wq = jnp.concatenate([uq_nope.reshape(LORA, D_MODEL),
                          jnp.concatenate([uq_rope, zr], axis=-1).reshape(LORA, D_MODEL),
                          jnp.concatenate([_rot_partner(uq_rope), zr], axis=-1).reshape(LORA, D_MODEL)],
                         axis=1).astype(BF16)
    wkv = jnp.concatenate([a_w_uk.reshape(LORA, D_MODEL), a_w_uv.reshape(LORA, D_MODEL)], axis=1).astype(BF16)
    w_uk_t = a_w_uk.transpose(1, 2, 0).reshape(D_MODEL, LORA).astype(BF16)
    w_uv2 = a_w_uv.reshape(LORA, D_MODEL).astype(BF16)
    w_out = a_w_out.astype(BF16)
    g_cq, g_ckv = a_g_cq.reshape(1, LORA), a_g_ckv.reshape(1, LORA)
    cos_p, sin_p = _rope_tables(pos_p)
    cos_s, sin_s = _rope_tables(pos_s)

    pa = proj(xpb, w_in)
    q, ckvn_p, krr_p, kk, vv = _mla_mid(pa, g_cq, g_ckv, cos_p, sin_p, wq, wkv)
    o = _flash_prompt(q, kk, vv, batch=batch, seq=seq, group=1, dqk=QK_PAD, dv=HEAD_DIM, q_cb=0, k_cb=0, v_cb=0,
                      scale=(HEAD_DIM + ROPE) ** -0.5)
    xp, xpb = _out_ln(o, pa, xp, w_out, ln_g[0], ln_b[0])

    sa = proj(xsb, w_in)
    q_s, ckvn_s, krr_s = _mla_mid(sa, g_cq, g_ckv, cos_s, sin_s, wq, None)
    ql = _head_matmul(q_s, w_uk_t, x_width=HEAD_DIM, x_stride=2, w_rows=True, out_width=LORA, out_dtype=BF16)
    ql = ql.reshape(nb, rows, LORA)
    qr = q_s.reshape(ms, N_HEADS, QK_PAD)[:, :, HEAD_DIM:].reshape(nb, rows, LANE)
    o_lat = _mla_decode(pt_flat, ql, qr, _pad_new_keys(ckvn_s, nb, n_tok), _pad_new_keys(krr_s, nb, n_tok),
                        cache_a_ckv, keys_last(cache_a_krope), n_pages, n_pg)
    o_s = _head_matmul(o_lat.reshape(ms, N_HEADS * LORA), w_uv2, x_width=LORA, x_stride=1, w_rows=False,
                       out_width=HEAD_DIM, out_dtype=F32)
    xs, xsb = _out_ln(o_s, sa, xs, w_out, ln_g[0], ln_b[0])
    a_ckv_p = ckvn_p.reshape(batch, seq, LORA)
    a_kr_p = krr_p[:, :ROPE].reshape(batch, seq, ROPE)
    a_ckv_s = ckvn_s.reshape(nb, n_tok, LORA)
    a_kr_s = krr_s[:, :ROPE].reshape(nb, n_tok, ROPE)

    kq, kk_, kv_ = D_MODEL, D_MODEL + KV_WIDTH, D_MODEL + 2 * KV_WIDTH
    w_in = jnp.concatenate([b_w_in[:, kv_:], b_w_in[:, :kv_]], axis=1).astype(BF16)
    w_out = b_w_out.astype(BF16)
    k_off, v_off = 2 * D_MODEL, 2 * D_MODEL + KV_WIDTH
    pb = proj(xpb, w_in)
    o = _sb_prompt(pb, batch=batch, seq=seq, q_cb=D_MODEL // KV_WIDTH, k_cb=k_off // HEAD_DIM, v_cb=v_off // HEAD_DIM)
    xp, xpb = _out_ln(o, pb, xp, w_out, ln_g[1], ln_b[1])
    sb = proj(xsb, w_in)
    k_new, v_new = sb[:, k_off:v_off], sb[:, v_off:v_off + KV_WIDTH]
    o_s = _sb_decode(pt_flat, _block_diag_queries(sb[:, D_MODEL:k_off], nb, n_tok),
                     _pad_new_keys(k_new, nb, n_tok), _pad_new_keys(v_new, nb, n_tok),
                     kv_rows(cache_b_k), kv_rows(cache_b_v),
                     n_pages, n_pg)
    xs, xsb = _out_ln(o_s.reshape(ms, D_MODEL), sb, xs, w_out, ln_g[1], ln_b[1])
    kv_shape_p = (batch, seq, KV_HEADS, HEAD_DIM)
    kv_shape_s = (nb, n_tok, KV_HEADS, HEAD_DIM)
    b_k_p = pb[:, k_off:v_off].reshape(kv_shape_p)
    b_v_p = pb[:, v_off:v_off + KV_WIDTH].reshape(kv_shape_p)
    b_k_s, b_v_s = k_new.reshape(kv_shape_s), v_new.reshape(kv_shape_s)

    f_off = v_off + KV_WIDTH
    w_f = c_w_in[:, kv_:kv_ + N_HEADS]
    w_in = jnp.concatenate([c_w_in[:, kv_ + N_HEADS:], c_w_in[:, :kv_], w_f, _zero_cols(w_f, 2 * LANE - N_HEADS)],
                           axis=1).astype(BF16)
    w_out = c_w_out.astype(BF16)
    bf_pad = jnp.pad(c_b_f, (0, LANE - N_HEADS)).reshape(1, LANE)
    pc = proj(xpb, w_in)
    lf_p, fc_p = _fox_gates(pc, bf_pad, batch=batch, seq=seq, f_cb=f_off // LANE)
    fk, fq = _gate_layouts(fc_p[:, :N_HEADS], batch, seq)
    o = _flash_prompt(pc, pc, pc, batch=batch, seq=seq, group=GROUP, dqk=HEAD_DIM, dv=HEAD_DIM,
                      q_cb=D_MODEL // HEAD_DIM, k_cb=k_off // HEAD_DIM, v_cb=v_off // HEAD_DIM,
                      scale=HEAD_DIM ** -0.5, fq=fq, fk=fk)
    xp, xpb = _out_ln(o, pc, xp, w_out, ln_g[2], ln_b[2])
    sc = proj(xsb, w_in)
    lf_s, _ = _fox_gates(sc, bf_pad, batch=1, seq=ms, f_cb=f_off // LANE)
    lf_new = lf_s[:, :N_HEADS].reshape(nb, n_tok, N_HEADS)
    lfn = jnp.broadcast_to(lf_new.transpose(0, 2, 1)[:, None], (nb, n_tok, N_HEADS, n_tok)).reshape(nb, rows, n_tok)
    lfn = jnp.pad(lfn, ((0, 0), (0, 0), (0, LANE - n_tok)))
    k_new, v_new = sc[:, k_off:v_off], sc[:, v_off:v_off + KV_WIDTH]
    o_s = _fox_decode(pt_flat, _block_diag_queries(sc[:, D_MODEL:k_off], nb, n_tok),
                      _pad_new_keys(k_new, nb, n_tok), _pad_new_keys(v_new, nb, n_tok), lfn,
                      kv_rows(cache_c_k), kv_rows(cache_c_v), keys_last(cache_c_logf), n_pages, n_pg, n_tok)
    xs, xsb = _out_ln(o_s.reshape(ms, D_MODEL), sc, xs, w_out, ln_g[2], ln_b[2])
    c_k_p = pc[:, k_off:v_off].reshape(kv_shape_p)
    c_v_p = pc[:, v_off:v_off + KV_WIDTH].reshape(kv_shape_p)
    c_logf_p = lf_p[:, :N_HEADS].reshape(batch, seq, N_HEADS)
    c_k_s, c_v_s = k_new.reshape(kv_shape_s), v_new.reshape(kv_shape_s)
    c_logf_s = lf_new

    hq, hv = ML_HEADS * ML_QK, ML_HEADS * ML_V
    w_q, w_k, w_v = d_w_in[:, :hq], d_w_in[:, hq:2 * hq], d_w_in[:, 2 * hq:2 * hq + hv]
    w_if = d_w_in[:, 2 * hq + hv:2 * hq + hv + 2 * ML_HEADS]
    w_op = d_w_in[:, 2 * hq + hv + 2 * ML_HEADS:2 * hq + 2 * hv + 2 * ML_HEADS]
    w_gate = d_w_in[:, 2 * hq + 2 * hv + 2 * ML_HEADS:]
    w_in = jnp.concatenate([w_gate, w_op, w_v, w_q, w_k, w_if, _zero_cols(w_if, 2 * LANE - 2 * ML_HEADS)],
                           axis=1).astype(BF16)
    w_out = d_w_out.astype(BF16)
    op_off, vd_off, qd_off, kd_off, if_off = D_MODEL, 2 * D_MODEL, 3 * D_MODEL, 3 * D_MODEL + hq, 3 * D_MODEL + 2 * hq
    cols = dict(q_cb=qd_off // ML_QK, k_cb=kd_off // ML_QK, v_cb=vd_off // ML_V, op_cb=op_off // ML_V)

    def ml_gates(p, n_seq, n_t):
        i_row, i_col = _gate_layouts(p[:, if_off:if_off + ML_HEADS], n_seq, n_t)
        f_row, f_col = _gate_layouts(p[:, if_off + ML_HEADS:if_off + 2 * ML_HEADS], n_seq, n_t)
        return i_row, f_row, i_col, f_col

    pd = proj(xpb, w_in)
    chunk = min(256, seq)
    n_chunks = seq // chunk
    h, d_C_p, d_n_p, d_m_p = _mlstm(
        pd.reshape(batch * n_chunks, chunk, pd.shape[1]), ml_gates(pd, batch, seq), d_b_i, d_b_f,
        jnp.zeros((batch, ML_HEADS, ML_QK, ML_V), F32), jnp.zeros((batch, ML_HEADS, 1, ML_QK), F32),
        jnp.full((batch, ML_HEADS, 1, 1), NEG_INF, F32), batch=batch, n_chunks=n_chunks, chunk=chunk, **cols)
    xp, xpb = _out_ln(h.reshape(mp, D_MODEL), pd, xp, w_out, ln_g[3], ln_b[3])
    sd = proj(xsb, w_in)
    h_s, d_C_s, d_n_s, d_m_s = _mlstm(
        sd.reshape(nb, n_tok, sd.shape[1]), ml_gates(sd, nb, n_tok), d_b_i, d_b_f,
        state_d_C, state_d_n.reshape(nb, ML_HEADS, 1, ML_QK), state_d_m.reshape(nb, ML_HEADS, 1, 1),
        batch=nb, n_chunks=1, chunk=PAGE, **cols)
    xs, xsb = _out_ln(h_s.reshape(ms, D_MODEL), sd, xs, w_out, ln_g[3], ln_b[3])

    return (xp.reshape(batch, seq, D_MODEL), xs.reshape(nb, n_tok, D_MODEL), a_ckv_p, a_kr_p, a_ckv_s, a_kr_s,
            b_k_p, b_v_p, b_k_s, b_v_s, c_k_p, c_v_p, c_logf_p, c_k_s, c_v_s, c_logf_s,
            d_C_p, d_n_p.reshape(batch, ML_HEADS, ML_QK), d_m_p.reshape(batch, ML_HEADS),
            d_C_s, d_n_s.reshape(nb, ML_HEADS, ML_QK), d_m_s.reshape(nb, ML_HEADS))
```
